```python
import jax, jax.numpy as jnp
from jax import lax
import numpy as np

D_MODEL = 1024
BATCH = 8
SEQ = 2048
DEPTH = 4
DEC_BATCH = 128
DEC_SEQ = 1
PAST_LEN = 16384
PAGE_SIZE = 128

CHUNK_A = 128
A_WIDTH = D_MODEL
A_HEADS = 8
A_HEAD_DIM = A_WIDTH // A_HEADS
B_HEADS = 4
B_KEY_DIM = D_MODEL // 2
B_VAL_DIM = D_MODEL
B_HK = B_KEY_DIM // B_HEADS
B_HV = B_VAL_DIM // B_HEADS
GATE_RANK = 16
GATE_NORMALIZER = 16.0
CHUNK_B = 64
N_MEM = 256
C_HEADS = 4
C_WIDTH = D_MODEL
C_HD = C_WIDTH // C_HEADS
N_BRANCH = 3
EPS = 1e-6
IN_SIZES = (A_WIDTH, A_WIDTH, A_WIDTH, B_KEY_DIM, B_KEY_DIM, B_VAL_DIM, GATE_RANK,
            B_VAL_DIM, C_WIDTH, C_WIDTH, N_BRANCH * D_MODEL)
IN_COLS = sum(IN_SIZES)

kernel_name = "gated_parallel_gmlp_gla_memxattn_step"


def rmsnorm(x, g):
    xf = x.astype(jnp.float32)
    y = xf * lax.rsqrt(jnp.mean(xf * xf, axis=-1, keepdims=True) + EPS)
    return (y * g.astype(jnp.float32)).astype(x.dtype)


def layernorm(x, g, b):
    xf = x.astype(jnp.float32)
    mu = jnp.mean(xf, axis=-1, keepdims=True)
    var = jnp.mean(jnp.square(xf - mu), axis=-1, keepdims=True)
    y = (xf - mu) * lax.rsqrt(var + EPS)
    return (y * g.astype(jnp.float32) + b.astype(jnp.float32)).astype(x.dtype)


def split_proj(h, w_in):
    offs = np.cumsum(IN_SIZES)[:-1].tolist()
    return jnp.split(h @ w_in, offs, axis=-1)


def spatial_mix(vn, w_s, b_s, n_pos):
    wm = jnp.tril(w_s[:, :n_pos, :n_pos])
    s = jnp.einsum('hts,bnshc->bnthc', wm, vn)
    return s + b_s[:, :n_pos].T[:, :, None].astype(s.dtype)


def gla_prep(b_q, b_k, b_v, b_gd, w_g_up, b_g):
    lead = b_q.shape[:-1]
    f32 = jnp.float32
    q = b_q.astype(f32).reshape(*lead, B_HEADS, B_HK) * (B_HK ** -0.5)
    k = b_k.astype(f32).reshape(*lead, B_HEADS, B_HK)
    v = b_v.astype(f32).reshape(*lead, B_HEADS, B_HV)
    glog = jax.nn.log_sigmoid((b_gd @ w_g_up + b_g).astype(f32)) / GATE_NORMALIZER
    return q, k, v, glog.reshape(*lead, B_HEADS, B_HK)


def gla_chunked(q, k, v, g):
    Bn, T, H, K = q.shape
    V = v.shape[-1]
    n = T // CHUNK_B
    rs = lambda a: a.reshape(Bn, n, CHUNK_B, H, a.shape[-1])
    q, k, v, g = rs(q), rs(k), rs(v), rs(g)
    bcum = jnp.cumsum(g, axis=2)
    q_dec = q * jnp.exp(bcum)
    k_dec = k * jnp.exp(-bcum)
    mask = jnp.tril(jnp.ones((CHUNK_B, CHUNK_B), dtype=bool))
    scores = jnp.where(mask, jnp.einsum('bnchk,bnjhk->bnhcj', q_dec, k_dec), 0.0)
    o_intra = jnp.einsum('bnhcj,bnjhv->bnchv', scores, v)
    b_last = bcum[:, :, -1]
    chunk_kv = jnp.einsum('bnchk,bnchv->bnhkv', k * jnp.exp(b_last[:, :, None] - bcum), v)

    def step(S, xs):
        qd, bl, kv = xs
        o_inter = jnp.einsum('bchk,bhkv->bchv', qd, S)
        return jnp.exp(bl)[..., None] * S + kv, o_inter

    S0 = jnp.zeros((Bn, H, K, V), jnp.float32)
    S_fin, o_inter = lax.scan(step, S0, (jnp.moveaxis(q_dec, 1, 0), jnp.moveaxis(b_last, 1, 0),
                                         jnp.moveaxis(chunk_kv, 1, 0)))
    o = o_intra + jnp.moveaxis(o_inter, 0, 1)
    return o.reshape(Bn, T, H, V), S_fin


def gla_recurrent(q, k, v, g, S0):
    def step(S, xs):
        qt, kt, vt, gt = xs
        S = jnp.exp(gt)[..., None] * S + kt[..., None] * vt[..., None, :]
        return S, jnp.einsum('bhk,bhkv->bhv', qt, S)
    S, o = lax.scan(step, S0.astype(jnp.float32),
                    (jnp.moveaxis(q, 1, 0), jnp.moveaxis(k, 1, 0), jnp.moveaxis(v, 1, 0), jnp.moveaxis(g, 1, 0)))
    return jnp.moveaxis(o, 0, 1), S


def gla_out(o, gn_g, b_z, dtype):
    lead = o.shape[:-2]
    on = rmsnorm(o, gn_g).reshape(*lead, B_VAL_DIM).astype(dtype)
    return on * jax.nn.silu(b_z)


def mem_kv(mem, g_mem, w_mem_kv):
    kv = rmsnorm(mem, g_mem) @ w_mem_kv
    mk, mv = jnp.split(kv, 2, axis=-1)
    Bn = mem.shape[0]
    return mk.reshape(Bn, N_MEM, C_HEADS, C_HD), mv.reshape(Bn, N_MEM, C_HEADS, C_HD)


def cross_attn(c_q, mk, mv):
    lead = c_q.shape[:-1]
    q = c_q.reshape(*lead, C_HEADS, C_HD)
    s = jnp.einsum('bthd,bmhd->bhtm', q, mk).astype(jnp.float32) * (C_HD ** -0.5)
    p = jax.nn.softmax(s, axis=-1).astype(mv.dtype)
    return jnp.einsum('bhtm,bmhd->bthd', p, mv).reshape(*lead, C_WIDTH)


def merge(gate_logits, b_gate, outs, w_branch, w_out):
    lead = gate_logits.shape[:-1]
    gates = jax.nn.sigmoid(gate_logits + b_gate).reshape(*lead, N_BRANCH, D_MODEL)
    br = jnp.stack(outs, axis=-2)
    proj = jnp.einsum('btic,icd->btid', br, w_branch)
    return jnp.sum(gates * proj, axis=-2) @ w_out


def layer_prompt(x, mem, l, p):
    (g_norm, w_in, b_gate, w_s, b_s, ln_v_g, ln_v_b, w_g_up, b_g, gn_g,
     g_mem, w_mem_kv, w_branch, w_out) = p
    Bn, T, _ = x.shape
    h = rmsnorm(x, g_norm[l])
    a_u, a_v, a_z, b_q, b_k, b_v, b_gd, b_z, c_q, c_z, gl = split_proj(h, w_in[l])
    a_u, a_v = jax.nn.gelu(a_u), jax.nn.gelu(a_v)
    vn = layernorm(a_v, ln_v_g[l], ln_v_b[l]).reshape(Bn, T // CHUNK_A, CHUNK_A, A_HEADS, A_HEAD_DIM)
    s = spatial_mix(vn, w_s[l], b_s[l], CHUNK_A).reshape(Bn, T, A_WIDTH)
    out_a = a_u * s * jax.nn.silu(a_z)
    q, k, v, g = gla_prep(b_q, b_k, b_v, b_gd, w_g_up[l], b_g[l])
    o, S = gla_chunked(q, k, v, g)
    out_b = gla_out(o, gn_g[l], b_z, x.dtype)
    mk, mv = mem_kv(mem, g_mem[l], w_mem_kv[l])
    out_c = cross_attn(c_q, mk, mv) * jax.nn.silu(c_z)
    x = x + merge(gl, b_gate[l], (out_a, out_b, out_c), w_branch[l], w_out[l])
    return x, S, mk, mv


def layer_sample(x, S0, mk, mv, l, p):
    (g_norm, w_in, b_gate, w_s, b_s, ln_v_g, ln_v_b, w_g_up, b_g, gn_g,
     g_mem, w_mem_kv, w_branch, w_out) = p
    Bn, T, _ = x.shape
    h = rmsnorm(x, g_norm[l])
    a_u, a_v, a_z, b_q, b_k, b_v, b_gd, b_z, c_q, c_z, gl = split_proj(h, w_in[l])
    a_u, a_v = jax.nn.gelu(a_u), jax.nn.gelu(a_v)
    vn = layernorm(a_v, ln_v_g[l], ln_v_b[l]).reshape(Bn, T, A_HEADS, A_HEAD_DIM)
    s = spatial_mix(vn[:, None], w_s[l], b_s[l], T)[:, 0].reshape(Bn, T, A_WIDTH)
    out_a = a_u * s * jax.nn.silu(a_z)
    q, k, v, g = gla_prep(b_q, b_k, b_v, b_gd, w_g_up[l], b_g[l])
    o, S = gla_recurrent(q, k, v, g, S0)
    out_b = gla_out(o, gn_g[l], b_z, x.dtype)
    out_c = cross_attn(c_q, mk, mv) * jax.nn.silu(c_z)
    x = x + merge(gl, b_gate[l], (out_a, out_b, out_c), w_branch[l], w_out[l])
    return x, S, vn


def setup_inputs(seed: int = 0) -> dict:
    key = jax.random.key(seed)
    ks = jax.random.split(key, 24)
    f32 = jnp.float32
    nrm = lambda k, shape, sc: jax.random.normal(k, shape, f32) * sc
    return {
        "x_prompt": nrm(ks[0], (BATCH, SEQ, D_MODEL), 1.0),
        "x_sample": nrm(ks[1], (DEC_BATCH, DEC_SEQ, D_MODEL), 1.0),
        "state_gla": nrm(ks[2], (DEPTH, DEC_BATCH, B_HEADS, B_HK, B_HV), 0.5),
        "cache_mem_k": nrm(ks[3], (DEPTH, DEC_BATCH, N_MEM, C_HEADS, C_HD), 1.0),
        "cache_mem_v": nrm(ks[4], (DEPTH, DEC_BATCH, N_MEM, C_HEADS, C_HD), 1.0),
        "mem_prompt": nrm(ks[5], (BATCH, N_MEM, D_MODEL), 1.0),
        "g_norm": 1.0 + nrm(ks[6], (DEPTH, D_MODEL), 0.05),
        "w_in": nrm(ks[7], (DEPTH, D_MODEL, IN_COLS), D_MODEL ** -0.5),
        "b_gate": nrm(ks[8], (DEPTH, N_BRANCH * D_MODEL), 0.1),
        "w_s": nrm(ks[9], (DEPTH, A_HEADS, CHUNK_A, CHUNK_A), CHUNK_A ** -0.5),
        "b_s": 1.0 + nrm(ks[10], (DEPTH, A_HEADS, CHUNK_A), 0.1),
        "ln_v_g": 1.0 + nrm(ks[11], (DEPTH, A_WIDTH), 0.05),
        "ln_v_b": nrm(ks[12], (DEPTH, A_WIDTH), 0.05),
        "w_g_up": nrm(ks[13], (DEPTH, GATE_RANK, B_KEY_DIM), GATE_RANK ** -0.5),
        "b_g": nrm(ks[14], (DEPTH, B_KEY_DIM), 0.1),
        "gn_g": 1.0 + nrm(ks[15], (DEPTH, B_HV), 0.05),
        "g_mem": 1.0 + nrm(ks[16], (DEPTH, D_MODEL), 0.05),
        "w_mem_kv": nrm(ks[17], (DEPTH, D_MODEL, 2 * C_WIDTH), D_MODEL ** -0.5),
        "w_branch": nrm(ks[18], (DEPTH, N_BRANCH, A_WIDTH, D_MODEL), A_WIDTH ** -0.5),
        "w_out": nrm(ks[19], (DEPTH, D_MODEL, D_MODEL), D_MODEL ** -0.5),
        "g_final": 1.0 + nrm(ks[20], (D_MODEL,), 0.05),
    }


def reference(x_prompt, x_sample, state_gla, cache_mem_k, cache_mem_v, mem_prompt,
              g_norm, w_in, b_gate, w_s, b_s, ln_v_g, ln_v_b, w_g_up, b_g, gn_g,
              g_mem, w_mem_kv, w_branch, w_out, g_final):
    p = (g_norm, w_in, b_gate, w_s, b_s, ln_v_g, ln_v_b, w_g_up, b_g, gn_g,
         g_mem, w_mem_kv, w_branch, w_out)
    xp, xs = x_prompt, x_sample
    gla_p, mk_p, mv_p, gla_s, v_s = [], [], [], [], []
    for l in range(DEPTH):
        xp, S_p, mk, mv = layer_prompt(xp, mem_prompt, l, p)
        gla_p.append(S_p)
        mk_p.append(mk)
        mv_p.append(mv)
        xs, S_s, vn = layer_sample(xs, state_gla[l], cache_mem_k[l], cache_mem_v[l], l, p)
        gla_s.append(S_s)
        v_s.append(vn)
    y_prompt = rmsnorm(xp, g_final)
    y_sample = rmsnorm(xs, g_final)
    state_gla_prompt = jnp.stack(gla_p).astype(x_prompt.dtype)
    cache_mem_k_prompt = jnp.stack(mk_p)
    cache_mem_v_prompt = jnp.stack(mv_p)
    state_gla_sample = jnp.stack(gla_s).astype(x_sample.dtype)
    state_gmlp_v_sample = jnp.stack(v_s)
    return (y_prompt, y_sample, state_gla_prompt, cache_mem_k_prompt, cache_mem_v_prompt,
            state_gla_sample, state_gmlp_v_sample)
```

```python
import functools

import jax
import jax.numpy as jnp
from jax import lax
from jax.experimental import pallas as pl
from jax.experimental.pallas import tpu as pltpu

F32 = jnp.float32
BF16 = jnp.bfloat16

D_MODEL = 1024
DEPTH = 4
CHUNK_A = 128
A_HEADS = 8
A_HEAD_DIM = 128
B_HEADS = 4
B_KEY_DIM = 512
B_HK = 128
B_HV = 256
GATE_RANK = 16
GATE_RANK_PAD = 128
INV_GATE_NORMALIZER = 1.0 / 16.0
CHUNK_B = 64
N_MEM = 256
C_HEADS = 4
C_HD = 256
EPS = 1e-6

VMEM_LIMIT_BYTES = 56 * 1024 * 1024
TM = 256
TM_MEM = 512
TM_NORM = 512
BT_GLA = 8
BT_ATT = 4


def _dot(a, b):
    return jnp.dot(a, b, preferred_element_type=F32)


def _dot_nt(a, b):
    return lax.dot_general(a, b, (((1,), (1,)), ((), ())), preferred_element_type=F32)


def _dot_tn(a, b):
    return lax.dot_general(a, b, (((0,), (0,)), ((), ())), preferred_element_type=F32)


def _sigmoid(x):
    return 1.0 / (1.0 + jnp.exp(-x))


def _silu(x):
    return x * _sigmoid(x)


def _gelu(x):
    c = 0.7978845608028654
    return x * (0.5 * (1.0 + jnp.tanh(c * (x + 0.044715 * (x * x * x)))))


def _log_sigmoid(x):
    return jnp.minimum(x, 0.0) - jnp.log1p(jnp.exp(-jnp.abs(x)))


def _rms(x, g):
    return x * lax.rsqrt(jnp.mean(x * x, axis=-1, keepdims=True) + EPS) * g


def _head_rms(o, g_full):
    parts = []
    for hd in range(B_HEADS):
        oh = o[:, hd * B_HV:(hd + 1) * B_HV]
        parts.append(oh * lax.rsqrt(jnp.mean(oh * oh, axis=-1, keepdims=True) + EPS))
    return jnp.concatenate(parts, axis=1) * g_full


def _layernorm(v, g, b):
    mu = jnp.mean(v, axis=-1, keepdims=True)
    vc = v - mu
    var = jnp.mean(vc * vc, axis=-1, keepdims=True)
    return vc * lax.rsqrt(var + EPS) * g + b


def _resident(shape):
    nd = len(shape)
    return pl.BlockSpec(shape, lambda *_: (0,) * nd, pipeline_mode=pl.Buffered(1))


def _params(semantics=None):
    return pltpu.CompilerParams(dimension_semantics=semantics,
                                vmem_limit_bytes=VMEM_LIMIT_BYTES)


def _memkv_kernel(mem_ref, g_ref, w_ref, mk_ref, mv_ref, mk16_ref, mv16_ref):
    hn = _rms(mem_ref[...], g_ref[...]).astype(BF16)
    mk = _dot(hn, w_ref[:, :D_MODEL])
    mv = _dot(hn, w_ref[:, D_MODEL:])
    for hd in range(C_HEADS):
        hs = slice(hd * C_HD, (hd + 1) * C_HD)
        mk_ref[:, hd, :] = mk[:, hs]
        mv_ref[:, hd, :] = mv[:, hs]
    mk16_ref[...] = mk.astype(BF16)
    mv16_ref[...] = mv.astype(BF16)


def _memkv(mem, g_mem, w_kv):
    batch = mem.shape[0]
    out5 = jax.ShapeDtypeStruct((DEPTH, batch, N_MEM, C_HEADS, C_HD), F32)
    out16 = jax.ShapeDtypeStruct((DEPTH, batch * N_MEM, D_MODEL), BF16)
    spec5 = pl.BlockSpec((None, None, N_MEM, C_HEADS, C_HD), lambda l, b: (l, b, 0, 0, 0))
    spec16 = pl.BlockSpec((None, N_MEM, D_MODEL), lambda l, b: (l, b, 0))
    return pl.pallas_call(
        _memkv_kernel,
        grid=(DEPTH, batch),
        in_specs=[
            pl.BlockSpec((None, N_MEM, D_MODEL), lambda l, b: (b, 0, 0)),
            pl.BlockSpec((None, 1, D_MODEL), lambda l, b: (l, 0, 0)),
            pl.BlockSpec((None, D_MODEL, 2 * D_MODEL), lambda l, b: (l, 0, 0)),
        ],
        out_specs=[spec5, spec5, spec16, spec16],
        out_shape=[out5, out5, out16, out16],
        compiler_params=_params(("arbitrary", "arbitrary")),
        name="memkv",
    )(mem, g_mem, w_kv)


def _prompt_a_kernel(x_ref, gn_ref, wa_ref, lng_ref, lnb_ref, ws_ref, bs_ref,
                     wg_ref, bg_ref, wbr_ref, h_ref, m_ref, s_scr):
    n_chunks = TM // CHUNK_A
    h = _rms(x_ref[...], gn_ref[...]).astype(BF16)
    h_ref[...] = h
    v = _gelu(_dot(h, wa_ref[:, D_MODEL:2 * D_MODEL]))
    vn = _layernorm(v, lng_ref[...], lnb_ref[...]).astype(BF16)
    row = lax.broadcasted_iota(jnp.int32, (CHUNK_A, CHUNK_A), 0)
    col = lax.broadcasted_iota(jnp.int32, (CHUNK_A, CHUNK_A), 1)
    causal = row >= col
    for hd in range(A_HEADS):
        cs = slice(hd * A_HEAD_DIM, (hd + 1) * A_HEAD_DIM)
        w = jnp.where(causal, ws_ref[hd], 0.0).astype(BF16)
        vh = jnp.concatenate(
            [vn[c * CHUNK_A:(c + 1) * CHUNK_A, cs] for c in range(n_chunks)], axis=1)
        mixed = _dot(w, vh)
        for c in range(n_chunks):
            s_scr[c * CHUNK_A:(c + 1) * CHUNK_A, cs] = mixed[:, c * CHUNK_A:(c + 1) * CHUNK_A]
    u = _gelu(_dot(h, wa_ref[:, :D_MODEL]))
    z = _dot(h, wa_ref[:, 2 * D_MODEL:])
    bias = jnp.concatenate([bs_ref[...]] * n_chunks, axis=0)
    out_a = (u * (s_scr[...] + bias) * _silu(z)).astype(BF16)
    gate = _sigmoid(_dot(h, wg_ref[...]) + bg_ref[...])
    m_ref[...] = gate * _dot(out_a, wbr_ref[...])


def _prompt_a(x, gn, wa, lng, lnb, ws, bs_full, wg, bg, wbr):
    n = x.shape[0]
    tile = lambda i: (i, 0)
    return pl.pallas_call(
        _prompt_a_kernel,
        grid=(n // TM,),
        in_specs=[
            pl.BlockSpec((TM, D_MODEL), tile),
            _resident(gn.shape), _resident(wa.shape), _resident(lng.shape),
            _resident(lnb.shape), _resident(ws.shape), _resident(bs_full.shape),
            _resident(wg.shape), _resident(bg.shape), _resident(wbr.shape),
        ],
        out_specs=[pl.BlockSpec((TM, D_MODEL), tile), pl.BlockSpec((TM, D_MODEL), tile)],
        out_shape=[jax.ShapeDtypeStruct((n, D_MODEL), BF16),
                   jax.ShapeDtypeStruct((n, D_MODEL), F32)],
        scratch_shapes=[pltpu.VMEM((TM, D_MODEL), F32)],
        compiler_params=_params(("parallel",)),
        name="prompt_a",
    )(x, gn, wa, lng, lnb, ws, bs_full, wg, bg, wbr)


def _prompt_b_kernel(h_ref, ma_ref, wb_ref, wgup_ref, bgu_ref, gnf_ref,
                     wg_ref, bg_ref, wbr_ref, m_ref, st_ref, o_scr):
    @pl.when(pl.program_id(1) == 0)
    def _():
        st_ref[...] = jnp.zeros_like(st_ref)

    h = h_ref[...]
    q = _dot(h, wb_ref[:, 0:B_KEY_DIM]) * (B_HK ** -0.5)
    k = _dot(h, wb_ref[:, B_KEY_DIM:2 * B_KEY_DIM])
    v = _dot(h, wb_ref[:, 2 * B_KEY_DIM:2 * B_KEY_DIM + D_MODEL])
    gd = _dot(h, wb_ref[:, 2 * B_KEY_DIM + 2 * D_MODEL:]).astype(BF16)
    g = _log_sigmoid(_dot(gd, wgup_ref[...]) + bgu_ref[...]) * INV_GATE_NORMALIZER

    row = lax.broadcasted_iota(jnp.int32, (CHUNK_B, CHUNK_B), 0)
    col = lax.broadcasted_iota(jnp.int32, (CHUNK_B, CHUNK_B), 1)
    causal = row >= col
    tri = jnp.where(causal, 1.0, 0.0).astype(BF16)

    for c in range(TM // CHUNK_B):
        rs = slice(c * CHUNK_B, (c + 1) * CHUNK_B)
        gc = g[rs]
        g_hi = gc.astype(BF16)
        g_lo = (gc - g_hi.astype(F32)).astype(BF16)
        bcum = _dot(tri, g_hi) + _dot(tri, g_lo)
        b_last = bcum[CHUNK_B - 1:CHUNK_B, :]
        q_dec = (q[rs] * jnp.exp(bcum)).astype(BF16)
        k_dec = (k[rs] * jnp.exp(-bcum)).astype(BF16)
        k_tail = (k[rs] * jnp.exp(b_last - bcum)).astype(BF16)
        decay = jnp.exp(b_last)
        vc = v[rs].astype(BF16)
        for hd in range(B_HEADS):
            ks = slice(hd * B_HK, (hd + 1) * B_HK)
            vs = slice(hd * B_HV, (hd + 1) * B_HV)
            scores = jnp.where(causal, _dot_nt(q_dec[:, ks], k_dec[:, ks]), 0.0)
            state_t = st_ref[hd]
            o_scr[rs, vs] = (_dot(scores.astype(BF16), vc[:, vs])
                             + _dot_nt(q_dec[:, ks], state_t.astype(BF16)))
            st_ref[hd] = state_t * decay[:, ks] + _dot_tn(vc[:, vs], k_tail[:, ks])

    z = _dot(h, wb_ref[:, 2 * B_KEY_DIM + D_MODEL:2 * B_KEY_DIM + 2 * D_MODEL])
    out_b = (_head_rms(o_scr[...], gnf_ref[...]) * _silu(z)).astype(BF16)
    gate = _sigmoid(_dot(h, wg_ref[...]) + bg_ref[...])
    m_ref[...] = ma_ref[...] + gate * _dot(out_b, wbr_ref[...])


def _prompt_b(h, ma, wb, wgup, bgu, gnf, wg, bg, wbr, batch):
    n = h.shape[0]
    steps = n // batch // TM
    tile = lambda b, t: (b * steps + t, 0)
    return pl.pallas_call(
        _prompt_b_kernel,
        grid=(batch, steps),
        in_specs=[
            pl.BlockSpec((TM, D_MODEL), tile), pl.BlockSpec((TM, D_MODEL), tile),
            _resident(wb.shape), _resident(wgup.shape), _resident(bgu.shape),
            _resident(gnf.shape), _resident(wg.shape), _resident(bg.shape),
            _resident(wbr.shape),
        ],
        out_specs=[
            pl.BlockSpec((TM, D_MODEL), tile),
            pl.BlockSpec((None, B_HEADS, B_HV, B_HK), lambda b, t: (b, 0, 0, 0)),
        ],
        out_shape=[jax.ShapeDtypeStruct((n, D_MODEL), F32),
                   jax.ShapeDtypeStruct((batch, B_HEADS, B_HV, B_HK), F32)],
        scratch_shapes=[pltpu.VMEM((TM, D_MODEL), F32)],
        compiler_params=_params(("parallel", "arbitrary")),
        name="prompt_b",
    )(h, ma, wb, wgup, bgu, gnf, wg, bg, wbr)


def _prompt_c_kernel(x_ref, h_ref, mab_ref, mk_ref, mv_ref, wc_ref,
                     wg_ref, bg_ref, wbr_ref, wout_ref, xo_ref):
    h = h_ref[...]
    q = _dot(h, wc_ref[:, :D_MODEL]).astype(BF16)
    mk = mk_ref[...]
    mv = mv_ref[...]
    parts = []
    for hd in range(C_HEADS):
        hs = slice(hd * C_HD, (hd + 1) * C_HD)
        s = _dot_nt(q[:, hs], mk[:, hs]) * (C_HD ** -0.5)
        e = jnp.exp(s - jnp.max(s, axis=-1, keepdims=True))
        p = e / jnp.sum(e, axis=-1, keepdims=True)
        parts.append(_dot(p.astype(BF16), mv[:, hs]))
    z = _dot(h, wc_ref[:, D_MODEL:])
    out_c = (jnp.concatenate(parts, axis=1) * _silu(z)).astype(BF16)
    gate = _sigmoid(_dot(h, wg_ref[...]) + bg_ref[...])
    merged = mab_ref[...] + gate * _dot(out_c, wbr_ref[...])
    xo_ref[...] = x_ref[...] + _dot(merged.astype(BF16), wout_ref[...])


def _prompt_c(x, h, mab, mk_all, mv_all, layer, wc, wg, bg, wbr, wout, batch):
    n = x.shape[0]
    steps = n // batch // TM
    tile = lambda b, t: (b * steps + t, 0)
    mem = lambda b, t: (layer, b, 0)
    return pl.pallas_call(
        _prompt_c_kernel,
        grid=(batch, steps),
        in_specs=[
            pl.BlockSpec((TM, D_MODEL), tile), pl.BlockSpec((TM, D_MODEL), tile),
            pl.BlockSpec((TM, D_MODEL), tile),
            pl.BlockSpec((None, N_MEM, D_MODEL), mem), pl.BlockSpec((None, N_MEM, D_MODEL), mem),
            _resident(wc.shape), _resident(wg.shape), _resident(bg.shape),
            _resident(wbr.shape), _resident(wout.shape),
        ],
        out_specs=pl.BlockSpec((TM, D_MODEL), tile),
        out_shape=jax.ShapeDtypeStruct((n, D_MODEL), F32),
        compiler_params=_params(("parallel", "arbitrary")),
        name="prompt_c",
    )(x, h, mab, mk_all, mv_all, wc, wg, bg, wbr, wout)


def _norm_kernel(x_ref, g_ref, y_ref):
    y_ref[...] = _rms(x_ref[...], g_ref[...])


def _final_norm(x, g):
    n = x.shape[0]
    tm = min(TM_NORM, n)
    return pl.pallas_call(
        _norm_kernel,
        grid=(n // tm,),
        in_specs=[pl.BlockSpec((tm, D_MODEL), lambda i: (i, 0)), _resident(g.shape)],
        out_specs=pl.BlockSpec((tm, D_MODEL), lambda i: (i, 0)),
        out_shape=jax.ShapeDtypeStruct((n, D_MODEL), F32),
        compiler_params=_params(("parallel",)),
        name="final_norm",
    )(x, g)


def _sample_front_kernel(x_ref, gn_ref, wa_ref, wb_ref, wc_ref, lng_ref, lnb_ref,
                         ws0_ref, bs0_ref, wgup_ref, bgu_ref,
                         h_ref, vn_ref, oa_ref, q_ref, k_ref, eg_ref, v_ref,
                         szb_ref, cq_ref, szc_ref):
    h = _rms(x_ref[...], gn_ref[...]).astype(BF16)
    h_ref[...] = h
    u = _gelu(_dot(h, wa_ref[:, :D_MODEL]))
    vn = _layernorm(_gelu(_dot(h, wa_ref[:, D_MODEL:2 * D_MODEL])), lng_ref[...], lnb_ref[...])
    vn_ref[...] = vn
    z = _dot(h, wa_ref[:, 2 * D_MODEL:])
    oa_ref[...] = (u * (vn * ws0_ref[...] + bs0_ref[...]) * _silu(z)).astype(BF16)
    q_ref[...] = _dot(h, wb_ref[:, 0:B_KEY_DIM]) * (B_HK ** -0.5)
    k_ref[...] = _dot(h, wb_ref[:, B_KEY_DIM:2 * B_KEY_DIM])
    v_ref[...] = _dot(h, wb_ref[:, 2 * B_KEY_DIM:2 * B_KEY_DIM + D_MODEL])
    szb_ref[...] = _silu(_dot(h, wb_ref[:, 2 * B_KEY_DIM + D_MODEL:2 * B_KEY_DIM + 2 * D_MODEL]))
    gd = _dot(h, wb_ref[:, 2 * B_KEY_DIM + 2 * D_MODEL:]).astype(BF16)
    g = _log_sigmoid(_dot(gd, wgup_ref[...]) + bgu_ref[...]) * INV_GATE_NORMALIZER
    eg_ref[...] = jnp.exp(g)
    cq_ref[...] = _dot(h, wc_ref[:, :D_MODEL])
    szc_ref[...] = _silu(_dot(h, wc_ref[:, D_MODEL:]))


def _sample_front(x, gn, wa, wb, wc, lng, lnb, ws0, bs0, wgup, bgu):
    n = x.shape[0]
    wide = jax.ShapeDtypeStruct((n, D_MODEL), F32)
    half = jax.ShapeDtypeStruct((n, B_KEY_DIM), F32)
    wide16 = jax.ShapeDtypeStruct((n, D_MODEL), BF16)
    return pl.pallas_call(
        _sample_front_kernel,
        out_shape=[wide16, wide, wide16, half, half, half, wide, wide, wide, wide],
        compiler_params=_params(),
        name="sample_front",
    )(x, gn, wa, wb, wc, lng, lnb, ws0, bs0, wgup, bgu)


def _sample_gla_kernel(q_ref, k_ref, eg_ref, v_ref, s0_ref, s_ref, o_ref):
    pieces = []
    for ref in (q_ref, k_ref, eg_ref):
        for hd in range(B_HEADS):
            pieces.append(ref[:, hd * B_HK:(hd + 1) * B_HK])
    used = len(pieces) * BT_GLA
    pieces.append(jnp.zeros((B_HK - used, B_HK), F32))
    cols = jnp.concatenate(pieces, axis=0).T
    for j in range(BT_GLA):
        for hd in range(B_HEADS):
            vs = slice(hd * B_HV, (hd + 1) * B_HV)
            iq = hd * BT_GLA + j
            ik = (B_HEADS + hd) * BT_GLA + j
            ie = (2 * B_HEADS + hd) * BT_GLA + j
            state = (cols[:, ie:ie + 1] * s0_ref[j, hd]
                     + cols[:, ik:ik + 1] * v_ref[j:j + 1, vs])
            s_ref[j, hd] = state
            o_ref[j:j + 1, vs] = jnp.sum(cols[:, iq:iq + 1] * state, axis=0, keepdims=True)


def _sample_gla(q, k, eg, v, s0_all, layer):
    n = q.shape[0]
    row = lambda i: (i, 0)
    return pl.pallas_call(
        _sample_gla_kernel,
        grid=(n // BT_GLA,),
        in_specs=[
            pl.BlockSpec((BT_GLA, B_KEY_DIM), row), pl.BlockSpec((BT_GLA, B_KEY_DIM), row),
            pl.BlockSpec((BT_GLA, B_KEY_DIM), row), pl.BlockSpec((BT_GLA, D_MODEL), row),
            pl.BlockSpec((None, BT_GLA, B_HEADS, B_HK, B_HV), lambda i: (layer, i, 0, 0, 0)),
        ],
        out_specs=[pl.BlockSpec((BT_GLA, B_HEADS, B_HK, B_HV), lambda i: (i, 0, 0, 0)),
                   pl.BlockSpec((BT_GLA, D_MODEL), row)],
        out_shape=[jax.ShapeDtypeStruct(s0_all.shape[1:], F32),
                   jax.ShapeDtypeStruct((n, D_MODEL), F32)],
        compiler_params=_params(("parallel",)),
        name="sample_gla",
    )(q, k, eg, v, s0_all)


def _sample_att_kernel(cq_ref, mk_ref, mv_ref, o_ref):
    for j in range(BT_ATT):
        q = cq_ref[j]
        s = jnp.sum(mk_ref[j] * q[None], axis=-1, keepdims=True) * (C_HD ** -0.5)
        e = jnp.exp(s - jnp.max(s, axis=0, keepdims=True))
        p = e / jnp.sum(e, axis=0, keepdims=True)
        o_ref[j] = jnp.sum(p * mv_ref[j], axis=0)


def _sample_att(cq3, mk_all, mv_all, layer):
    n = cq3.shape[0]
    row = lambda i: (i, 0, 0)
    cache = lambda i: (layer, i, 0, 0, 0)
    return pl.pallas_call(
        _sample_att_kernel,
        grid=(n // BT_ATT,),
        in_specs=[
            pl.BlockSpec((BT_ATT, C_HEADS, C_HD), row),
            pl.BlockSpec((None, BT_ATT, N_MEM, C_HEADS, C_HD), cache),
            pl.BlockSpec((None, BT_ATT, N_MEM, C_HEADS, C_HD), cache),
        ],
        out_specs=pl.BlockSpec((BT_ATT, C_HEADS, C_HD), row),
        out_shape=jax.ShapeDtypeStruct((n, C_HEADS, C_HD), F32),
        compiler_params=_params(("parallel",)),
        name="sample_att",
    )(cq3, mk_all, mv_all)


def _sample_back_kernel(x_ref, h_ref, oa_ref, og_ref, szb_ref, oc_ref, szc_ref, gnf_ref,
                        wg_ref, bg_ref, wbr_ref, wout_ref, xo_ref):
    h = h_ref[...]
    out_b = (_head_rms(og_ref[...], gnf_ref[...]) * szb_ref[...]).astype(BF16)
    out_c = (oc_ref[...] * szc_ref[...]).astype(BF16)
    merged = None
    for i, br in enumerate((oa_ref[...], out_b, out_c)):
        cs = slice(i * D_MODEL, (i + 1) * D_MODEL)
        gate = _sigmoid(_dot(h, wg_ref[:, cs]) + bg_ref[:, cs])
        term = gate * _dot(br, wbr_ref[i])
        merged = term if merged is None else merged + term
    xo_ref[...] = x_ref[...] + _dot(merged.astype(BF16), wout_ref[...])


def _sample_back(x, h, oa, og, szb, oc, szc, gnf, wg, bg, wbr, wout):
    return pl.pallas_call(
        _sample_back_kernel,
        out_shape=jax.ShapeDtypeStruct(x.shape, F32),
        compiler_params=_params(),
        name="sample_back",
    )(x, h, oa, og, szb, oc, szc, gnf, wg, bg, wbr, wout)


def kernel(x_prompt, x_sample, state_gla, cache_mem_k, cache_mem_v, mem_prompt, g_norm, w_in, b_gate, w_s, b_s, ln_v_g, ln_v_b, w_g_up, b_g, gn_g, g_mem, w_mem_kv, w_branch, w_out, g_final):
    batch, seq, _ = x_prompt.shape
    dec_batch = x_sample.shape[0]
    row = lambda a: a.reshape(1, -1)

    o_bq = 3 * D_MODEL
    o_gd = o_bq + 2 * B_KEY_DIM + D_MODEL
    o_bz = o_gd + GATE_RANK
    o_cq = o_bz + D_MODEL
    o_gl = o_cq + 2 * D_MODEL

    mk_out, mv_out, mk_all, mv_all = _memkv(
        mem_prompt, g_mem.reshape(DEPTH, 1, D_MODEL), w_mem_kv.astype(BF16))

    xp = x_prompt.reshape(batch * seq, D_MODEL)
    xs = x_sample.reshape(dec_batch, D_MODEL)
    gla_p, gla_s, v_s = [], [], []
    for l in range(DEPTH):
        wl = w_in[l]
        wa = wl[:, :o_bq].astype(BF16)
        wb = jnp.concatenate(
            [wl[:, o_bq:o_gd], wl[:, o_bz:o_cq],
             jnp.pad(wl[:, o_gd:o_bz], ((0, 0), (0, GATE_RANK_PAD - GATE_RANK)))],
            axis=1).astype(BF16)
        wc = wl[:, o_cq:o_gl].astype(BF16)
        wg = wl[:, o_gl:].astype(BF16)
        wgup = jnp.pad(w_g_up[l], ((0, GATE_RANK_PAD - GATE_RANK), (0, 0))).astype(BF16)
        wbr = w_branch[l].astype(BF16)
        wout = w_out[l].astype(BF16)
        gn, lng, lnb, bgu = row(g_norm[l]), row(ln_v_g[l]), row(ln_v_b[l]), row(b_g[l])
        bg = row(b_gate[l])
        gnf = row(jnp.tile(gn_g[l], B_HEADS))
        bs_full = jnp.repeat(b_s[l].T, A_HEAD_DIM, axis=1)
        ws0 = row(jnp.repeat(w_s[l, :, 0, 0], A_HEAD_DIM))
        bs0 = row(jnp.repeat(b_s[l, :, 0], A_HEAD_DIM))

        h, m = _prompt_a(xp, gn, wa, lng, lnb, w_s[l], bs_full,
                         wg[:, :D_MODEL], bg[:, :D_MODEL], wbr[0])
        m, st = _prompt_b(h, m, wb, wgup, bgu, gnf,
                          wg[:, D_MODEL:2 * D_MODEL], bg[:, D_MODEL:2 * D_MODEL], wbr[1], batch)
        xp = _prompt_c(xp, h, m, mk_all, mv_all, l, wc,
                       wg[:, 2 * D_MODEL:], bg[:, 2 * D_MODEL:], wbr[2], wout, batch)
        gla_p.append(jnp.swapaxes(st, -1, -2))

        (hs_, vn, oa, q, k, eg, v, szb, cq, szc) = _sample_front(
            xs, gn, wa, wb, wc, lng, lnb, ws0, bs0, wgup, bgu)
        s_new, og = _sample_gla(q, k, eg, v, state_gla, l)
        oc = _sample_att(cq.reshape(dec_batch, C_HEADS, C_HD), cache_mem_k, cache_mem_v, l)
        xs = _sample_back(xs, hs_, oa, og, szb, oc.reshape(dec_batch, D_MODEL), szc, gnf,
                          wg, bg, wbr, wout)
        gla_s.append(s_new)
        v_s.append(vn)

    gf = row(g_final)
    y_prompt = _final_norm(xp, gf).reshape(batch, seq, D_MODEL)
    y_sample = _final_norm(xs, gf).reshape(dec_batch, 1, D_MODEL)
    return (y_prompt, y_sample, jnp.stack(gla_p), mk_out, mv_out, jnp.stack(gla_s),
            jnp.stack(v_s).reshape(DEPTH, dec_batch, 1, A_HEADS, A_HEAD_DIM))
```

```python
import functools

import jax
import jax.numpy as jnp
from jax import lax
from jax.experimental import pallas as pl
from jax.experimental.pallas import tpu as pltpu

F32 = jnp.float32
BF16 = jnp.bfloat16

D_MODEL = 1024
DEPTH = 4
CHUNK_A = 128
A_HEADS = 8
A_HEAD_DIM = 128
B_HEADS = 4
B_KEY_DIM = 512
B_HK = 128
B_HV = 256
GATE_RANK = 16
GATE_RANK_PAD = 128
INV_GATE_NORMALIZER = 1.0 / 16.0
CHUNK_B = 64
N_MEM = 256
C_HEADS = 4
C_HD = 256
N_BRANCH = 3
EPS = 1e-6

VMEM_LIMIT_BYTES = 56 * 1024 * 1024
TM = 512
BT_GLA = 8
BT_ATT = 4


def _dot(a, b):
    return jnp.dot(a, b, preferred_element_type=F32)


def _dot_nt(a, b):
    return lax.dot_general(a, b, (((1,), (1,)), ((), ())), preferred_element_type=F32)


def _dot_tn(a, b):
    return lax.dot_general(a, b, (((0,), (0,)), ((), ())), preferred_element_type=F32)


def _sigmoid(x):
    return 1.0 / (1.0 + jnp.exp(-x))


def _silu(x):
    return x * _sigmoid(x)


def _gelu(x):
    c = 0.7978845608028654
    return x * (0.5 * (1.0 + jnp.tanh(c * (x + 0.044715 * (x * x * x)))))


def _log_sigmoid(x):
    return jnp.minimum(x, 0.0) - jnp.log1p(jnp.exp(-jnp.abs(x)))


def _rms(x, g):
    return x * lax.rsqrt(jnp.mean(x * x, axis=-1, keepdims=True) + EPS) * g


def _head_rms(o, g_full):
    parts = []
    for hd in range(B_HEADS):
        oh = o[:, hd * B_HV:(hd + 1) * B_HV]
        parts.append(oh * lax.rsqrt(jnp.mean(oh * oh, axis=-1, keepdims=True) + EPS))
    return jnp.concatenate(parts, axis=1) * g_full


def _layernorm(v, g, b):
    mu = jnp.mean(v, axis=-1, keepdims=True)
    vc = v - mu
    var = jnp.mean(vc * vc, axis=-1, keepdims=True)
    return vc * lax.rsqrt(var + EPS) * g + b


def _resident(shape):
    nd = len(shape)
    return pl.BlockSpec(shape, lambda *_: (0,) * nd, pipeline_mode=pl.Buffered(1))


def _layer(arr, layer, cols=None, col_block=0):
    shape = arr.shape[1:] if cols is None else arr.shape[1:-1] + (cols,)
    idx = (layer,) + (0,) * (len(shape) - 1) + (col_block,)
    return pl.BlockSpec((None,) + shape, lambda *_: idx, pipeline_mode=pl.Buffered(1))


def _params(semantics=None):
    return pltpu.CompilerParams(dimension_semantics=semantics,
                                vmem_limit_bytes=VMEM_LIMIT_BYTES)


O_BQ = 3 * D_MODEL
O_GD = O_BQ + 2 * B_KEY_DIM + D_MODEL
O_BZ = O_GD + GATE_RANK
O_CQ = O_BZ + D_MODEL
O_GL = O_CQ + 2 * D_MODEL
IN_COLS = O_GL + 3 * D_MODEL
WB_COLS = 2 * B_KEY_DIM + 2 * D_MODEL + GATE_RANK_PAD
TR_PREP = 128


def _split_w_in_kernel(wt_ref, wa_ref, wb_ref, wc_ref, wg_ref):
    def seg(lo, hi):
        return wt_ref[lo:hi, :].T.astype(BF16)

    wa_ref[...] = seg(0, O_BQ)
    wb_ref[:, :O_GD - O_BQ] = seg(O_BQ, O_GD)
    wb_ref[:, O_GD - O_BQ:O_GD - O_BQ + D_MODEL] = seg(O_BZ, O_CQ)
    gd = wt_ref[O_GD:O_GD + GATE_RANK_PAD, :].T
    lane = lax.broadcasted_iota(jnp.int32, gd.shape, 1)
    wb_ref[:, O_GD - O_BQ + D_MODEL:] = jnp.where(lane < GATE_RANK, gd, 0.0).astype(BF16)
    wc_ref[...] = seg(O_CQ, O_GL)
    wg_ref[...] = seg(O_GL, IN_COLS)


def _split_w_in(w_in_t):
    depth, _, rows = w_in_t.shape
    widths = (O_BQ, WB_COLS, O_GL - O_CQ, IN_COLS - O_GL)
    return pl.pallas_call(
        _split_w_in_kernel,
        grid=(depth, rows // TR_PREP),
        in_specs=[pl.BlockSpec((None, IN_COLS, TR_PREP), lambda l, i: (l, 0, i))],
        out_specs=[pl.BlockSpec((None, TR_PREP, n), lambda l, i: (l, i, 0)) for n in widths],
        out_shape=[jax.ShapeDtypeStruct((depth, rows, n), BF16) for n in widths],
        compiler_params=_params(("parallel", "parallel")),
        name="split_w_in",
    )(w_in_t)


def _cast_kernel(w_ref, o_ref):
    o_ref[...] = w_ref[...].astype(BF16)


def _cast_bf16(w):
    n, rows, cols = w.shape
    spec = pl.BlockSpec((None, rows, cols), lambda i: (i, 0, 0))
    return pl.pallas_call(
        _cast_kernel,
        grid=(n,),
        in_specs=[spec],
        out_specs=spec,
        out_shape=jax.ShapeDtypeStruct(w.shape, BF16),
        compiler_params=_params(("parallel",)),
        name="cast_bf16",
    )(w)


def _memkv_kernel(mem_ref, g_ref, w_ref, mk_ref, mv_ref, mk16_ref, mv16_ref):
    hn = _rms(mem_ref[...], g_ref[...]).astype(BF16)
    mk = _dot(hn, w_ref[:, :D_MODEL])
    mv = _dot(hn, w_ref[:, D_MODEL:])
    for hd in range(C_HEADS):
        hs = slice(hd * C_HD, (hd + 1) * C_HD)
        mk_ref[:, hd, :] = mk[:, hs]
        mv_ref[:, hd, :] = mv[:, hs]
    mk16_ref[...] = mk.astype(BF16)
    mv16_ref[...] = mv.astype(BF16)


def _memkv(mem, g_mem, w_kv):
    batch = mem.shape[0]
    out5 = jax.ShapeDtypeStruct((DEPTH, batch, N_MEM, C_HEADS, C_HD), F32)
    out16 = jax.ShapeDtypeStruct((DEPTH, batch * N_MEM, D_MODEL), BF16)
    spec5 = pl.BlockSpec((None, None, N_MEM, C_HEADS, C_HD), lambda l, b: (l, b, 0, 0, 0))
    spec16 = pl.BlockSpec((None, N_MEM, D_MODEL), lambda l, b: (l, b, 0))
    return pl.pallas_call(
        _memkv_kernel,
        grid=(DEPTH, batch),
        in_specs=[
            pl.BlockSpec((None, N_MEM, D_MODEL), lambda l, b: (b, 0, 0)),
            pl.BlockSpec((None, 1, D_MODEL), lambda l, b: (l, 0, 0)),
            pl.BlockSpec((None, D_MODEL, 2 * D_MODEL), lambda l, b: (l, 0, 0)),
        ],
        out_specs=[spec5, spec5, spec16, spec16],
        out_shape=[out5, out5, out16, out16],
        compiler_params=_params(("arbitrary", "arbitrary")),
        name="memkv",
    )(mem, g_mem, w_kv)


def _prompt_a_kernel(x_ref, gn_ref, wa_ref, lng_ref, lnb_ref, ws_ref, bs_ref,
                     wg_ref, bg_ref, wbr_ref, h_ref, m_ref, s_scr):
    n_chunks = TM // CHUNK_A
    h = _rms(x_ref[...], gn_ref[...]).astype(BF16)
    h_ref[...] = h
    v = _gelu(_dot(h, wa_ref[:, D_MODEL:2 * D_MODEL]))
    vn = _layernorm(v, lng_ref[...], lnb_ref[...]).astype(BF16)
    row = lax.broadcasted_iota(jnp.int32, (CHUNK_A, CHUNK_A), 0)
    col = lax.broadcasted_iota(jnp.int32, (CHUNK_A, CHUNK_A), 1)
    causal = row >= col
    for hd in range(A_HEADS):
        cs = slice(hd * A_HEAD_DIM, (hd + 1) * A_HEAD_DIM)
        w = jnp.where(causal, ws_ref[hd], 0.0).astype(BF16)
        vh = jnp.concatenate(
            [vn[c * CHUNK_A:(c + 1) * CHUNK_A, cs] for c in range(n_chunks)], axis=1)
        mixed = _dot(w, vh)
        for c in range(n_chunks):
            s_scr[c * CHUNK_A:(c + 1) * CHUNK_A, cs] = mixed[:, c * CHUNK_A:(c + 1) * CHUNK_A]
    u = _gelu(_dot(h, wa_ref[:, :D_MODEL]))
    z = _dot(h, wa_ref[:, 2 * D_MODEL:])
    bias = jnp.concatenate([bs_ref[...]] * n_chunks, axis=0)
    out_a = (u * (s_scr[...] + bias) * _silu(z)).astype(BF16)
    gate = _sigmoid(_dot(h, wg_ref[...]) + bg_ref[...])
    m_ref[...] = gate * _dot(out_a, wbr_ref[...])


def _prompt_a(x, p, l):
    n = x.shape[0]
    tile = lambda i: (i, 0)
    args = (x, p["gn"], p["wa"], p["lng"], p["lnb"], p["ws"], p["bs_full"],
            p["wg"], p["bg"], p["wbr"])
    return pl.pallas_call(
        _prompt_a_kernel,
        grid=(n // TM,),
        in_specs=[
            pl.BlockSpec((TM, D_MODEL), tile),
            _layer(p["gn"], l), _layer(p["wa"], l), _layer(p["lng"], l),
            _layer(p["lnb"], l), _layer(p["ws"], l), _layer(p["bs_full"], l),
            _layer(p["wg"], l, D_MODEL, 0), _layer(p["bg"], l, D_MODEL, 0),
            _layer(p["wbr"], N_BRANCH * l + 0),
        ],
        out_specs=[pl.BlockSpec((TM, D_MODEL), tile), pl.BlockSpec((TM, D_MODEL), tile)],
        out_shape=[jax.ShapeDtypeStruct((n, D_MODEL), BF16),
                   jax.ShapeDtypeStruct((n, D_MODEL), F32)],
        scratch_shapes=[pltpu.VMEM((TM, D_MODEL), F32)],
        compiler_params=_params(("parallel",)),
        name="prompt_a",
    )(*args)


def _prompt_b_kernel(h_ref, ma_ref, wb_ref, wgup_ref, bgu_ref, gnf_ref,
                     wg_ref, bg_ref, wbr_ref, m_ref, st_ref, o_scr):
    @pl.when(pl.program_id(1) == 0)
    def _():
        st_ref[...] = jnp.zeros_like(st_ref)

    h = h_ref[...]
    q = _dot(h, wb_ref[:, 0:B_KEY_DIM]) * (B_HK ** -0.5)
    k = _dot(h, wb_ref[:, B_KEY_DIM:2 * B_KEY_DIM])
    v = _dot(h, wb_ref[:, 2 * B_KEY_DIM:2 * B_KEY_DIM + D_MODEL])
    gd = _dot(h, wb_ref[:, 2 * B_KEY_DIM + 2 * D_MODEL:]).astype(BF16)
    g = _log_sigmoid(_dot(gd, wgup_ref[...]) + bgu_ref[...]) * INV_GATE_NORMALIZER

    row = lax.broadcasted_iota(jnp.int32, (CHUNK_B, CHUNK_B), 0)
    col = lax.broadcasted_iota(jnp.int32, (CHUNK_B, CHUNK_B), 1)
    causal = row >= col
    tri = jnp.where(causal, 1.0, 0.0).astype(BF16)

    for c in range(TM // CHUNK_B):
        rs = slice(c * CHUNK_B, (c + 1) * CHUNK_B)
        gc = g[rs]
        g_hi = gc.astype(BF16)
        g_lo = (gc - g_hi.astype(F32)).astype(BF16)
        bcum = _dot(tri, g_hi) + _dot(tri, g_lo)
        b_last = bcum[CHUNK_B - 1:CHUNK_B, :]
        q_dec = (q[rs] * jnp.exp(bcum)).astype(BF16)
        k_dec = (k[rs] * jnp.exp(-bcum)).astype(BF16)
        k_tail = (k[rs] * jnp.exp(b_last - bcum)).astype(BF16)
        decay = jnp.exp(b_last)
        vc = v[rs].astype(BF16)
        for hd in range(B_HEADS):
            ks = slice(hd * B_HK, (hd + 1) * B_HK)
            vs = slice(hd * B_HV, (hd + 1) * B_HV)
            scores = jnp.where(causal, _dot_nt(q_dec[:, ks], k_dec[:, ks]), 0.0)
            state_t = st_ref[hd]
            o_scr[rs, vs] = (_dot(scores.astype(BF16), vc[:, vs])
                             + _dot_nt(q_dec[:, ks], state_t.astype(BF16)))
            st_ref[hd] = state_t * decay[:, ks] + _dot_tn(vc[:, vs], k_tail[:, ks])

    z = _dot(h, wb_ref[:, 2 * B_KEY_DIM + D_MODEL:2 * B_KEY_DIM + 2 * D_MODEL])
    out_b = (_head_rms(o_scr[...], gnf_ref[...]) * _silu(z)).astype(BF16)
    gate = _sigmoid(_dot(h, wg_ref[...]) + bg_ref[...])
    m_ref[...] = ma_ref[...] + gate * _dot(out_b, wbr_ref[...])


def _prompt_b(h, ma, p, l, batch):
    n = h.shape[0]
    steps = n // batch // TM
    tile = lambda b, t: (b * steps + t, 0)
    args = (h, ma, p["wb"], p["wgup"], p["bgu"], p["gnf"], p["wg"], p["bg"], p["wbr"])
    return pl.pallas_call(
        _prompt_b_kernel,
        grid=(batch, steps),
        in_specs=[
            pl.BlockSpec((TM, D_MODEL), tile), pl.BlockSpec((TM, D_MODEL), tile),
            _layer(p["wb"], l), _layer(p["wgup"], l), _layer(p["bgu"], l),
            _layer(p["gnf"], l),
            _layer(p["wg"], l, D_MODEL, 1), _layer(p["bg"], l, D_MODEL, 1),
            _layer(p["wbr"], N_BRANCH * l + 1),
        ],
        out_specs=[
            pl.BlockSpec((TM, D_MODEL), tile),
            pl.BlockSpec((None, B_HEADS, B_HV, B_HK), lambda b, t: (b, 0, 0, 0)),
        ],
        out_shape=[jax.ShapeDtypeStruct((n, D_MODEL), F32),
                   jax.ShapeDtypeStruct((batch, B_HEADS, B_HV, B_HK), F32)],
        scratch_shapes=[pltpu.VMEM((TM, D_MODEL), F32)],
        compiler_params=_params(("parallel", "arbitrary")),
        name="prompt_b",
    )(*args)


def _prompt_c_kernel(x_ref, h_ref, mab_ref, mk_ref, mv_ref, wc_ref,
                     wg_ref, bg_ref, wbr_ref, wout_ref, gf_ref, xo_ref, *, final):
    h = h_ref[...]
    q = _dot(h, wc_ref[:, :D_MODEL]).astype(BF16)
    mk = mk_ref[...]
    mv = mv_ref[...]
    parts = []
    for hd in range(C_HEADS):
        hs = slice(hd * C_HD, (hd + 1) * C_HD)
        s = _dot_nt(q[:, hs], mk[:, hs]) * (C_HD ** -0.5)
        e = jnp.exp(s - jnp.max(s, axis=-1, keepdims=True))
        p = e / jnp.sum(e, axis=-1, keepdims=True)
        parts.append(_dot(p.astype(BF16), mv[:, hs]))
    z = _dot(h, wc_ref[:, D_MODEL:])
    out_c = (jnp.concatenate(parts, axis=1) * _silu(z)).astype(BF16)
    gate = _sigmoid(_dot(h, wg_ref[...]) + bg_ref[...])
    merged = mab_ref[...] + gate * _dot(out_c, wbr_ref[...])
    x_new = x_ref[...] + _dot(merged.astype(BF16), wout_ref[...])
    xo_ref[...] = _rms(x_new, gf_ref[...]) if final else x_new


def _prompt_c(x, h, mab, mk_all, mv_all, p, l, batch):
    n = x.shape[0]
    steps = n // batch // TM
    tile = lambda b, t: (b * steps + t, 0)
    mem = lambda b, t: (l, b, 0)
    args = (x, h, mab, mk_all, mv_all, p["wc"], p["wg"], p["bg"], p["wbr"], p["wout"], p["gf"])
    return pl.pallas_call(
        functools.partial(_prompt_c_kernel, final=l == DEPTH - 1),
        grid=(batch, steps),
        in_specs=[
            pl.BlockSpec((TM, D_MODEL), tile), pl.BlockSpec((TM, D_MODEL), tile),
            pl.BlockSpec((TM, D_MODEL), tile),
            pl.BlockSpec((None, N_MEM, D_MODEL), mem), pl.BlockSpec((None, N_MEM, D_MODEL), mem),
            _layer(p["wc"], l),
            _layer(p["wg"], l, D_MODEL, 2), _layer(p["bg"], l, D_MODEL, 2),
            _layer(p["wbr"], N_BRANCH * l + 2), _layer(p["wout"], l),
            _resident(p["gf"].shape),
        ],
        out_specs=pl.BlockSpec((TM, D_MODEL), tile),
        out_shape=jax.ShapeDtypeStruct((n, D_MODEL), F32),
        compiler_params=_params(("parallel", "arbitrary")),
        name="prompt_c",
    )(*args)


def _sample_front_kernel(x_ref, gn_ref, wa_ref, wb_ref, wc_ref, lng_ref, lnb_ref,
                         ws0_ref, bs0_ref, wgup_ref, bgu_ref,
                         h_ref, vn_ref, oa_ref, q_ref, k_ref, eg_ref, v_ref,
                         szb_ref, cq_ref, szc_ref):
    h = _rms(x_ref[...], gn_ref[...]).astype(BF16)
    h_ref[...] = h
    u = _gelu(_dot(h, wa_ref[:, :D_MODEL]))
    vn = _layernorm(_gelu(_dot(h, wa_ref[:, D_MODEL:2 * D_MODEL])), lng_ref[...], lnb_ref[...])
    vn_ref[...] = vn
    z = _dot(h, wa_ref[:, 2 * D_MODEL:])
    oa_ref[...] = (u * (vn * ws0_ref[...] + bs0_ref[...]) * _silu(z)).astype(BF16)
    q_ref[...] = _dot(h, wb_ref[:, 0:B_KEY_DIM]) * (B_HK ** -0.5)
    k_ref[...] = _dot(h, wb_ref[:, B_KEY_DIM:2 * B_KEY_DIM])
    v_ref[...] = _dot(h, wb_ref[:, 2 * B_KEY_DIM:2 * B_KEY_DIM + D_MODEL])
    szb_ref[...] = _silu(_dot(h, wb_ref[:, 2 * B_KEY_DIM + D_MODEL:2 * B_KEY_DIM + 2 * D_MODEL]))
    gd = _dot(h, wb_ref[:, 2 * B_KEY_DIM + 2 * D_MODEL:]).astype(BF16)
    g = _log_sigmoid(_dot(gd, wgup_ref[...]) + bgu_ref[...]) * INV_GATE_NORMALIZER
    eg_ref[...] = jnp.exp(g)
    cq_ref[...] = _dot(h, wc_ref[:, :D_MODEL]) * (C_HD ** -0.5)
    szc_ref[...] = _silu(_dot(h, wc_ref[:, D_MODEL:]))


def _sample_front(x, p, l):
    n = x.shape[0]
    wide = jax.ShapeDtypeStruct((n, D_MODEL), F32)
    half = jax.ShapeDtypeStruct((n, B_KEY_DIM), F32)
    wide16 = jax.ShapeDtypeStruct((n, D_MODEL), BF16)
    out_shape = [wide16, wide, wide16, half, half, half, wide, wide, wide, wide]
    names = ("gn", "wa", "wb", "wc", "lng", "lnb", "ws0", "bs0", "wgup", "bgu")
    return pl.pallas_call(
        _sample_front_kernel,
        grid=(1,),
        in_specs=[_resident(x.shape)] + [_layer(p[k], l) for k in names],
        out_specs=[pl.BlockSpec(s.shape, lambda i: (0, 0)) for s in out_shape],
        out_shape=out_shape,
        compiler_params=_params(("arbitrary",)),
        name="sample_front",
    )(x, *(p[k] for k in names))


def _sample_gla_kernel(q_ref, k_ref, eg_ref, v_ref, s0_ref, *rest):
    s_ref, o_ref = rest[-2:]

    @pl.when(pl.program_id(0) == 0)
    def _():
        pieces = []
        for ref in (q_ref, k_ref, eg_ref):
            for hd in range(B_HEADS):
                pieces.append(ref[:, hd * B_HK:(hd + 1) * B_HK])
        used = len(pieces) * BT_GLA
        pieces.append(jnp.zeros((B_HK - used, B_HK), F32))
        cols = jnp.concatenate(pieces, axis=0).T
        for j in range(BT_GLA):
            for hd in range(B_HEADS):
                vs = slice(hd * B_HV, (hd + 1) * B_HV)
                iq = hd * BT_GLA + j
                ik = (B_HEADS + hd) * BT_GLA + j
                ie = (2 * B_HEADS + hd) * BT_GLA + j
                state = (cols[:, ie:ie + 1] * s0_ref[j, hd]
                         + cols[:, ik:ik + 1] * v_ref[j:j + 1, vs])
                s_ref[j, hd] = state
                o_ref[j:j + 1, vs] = jnp.sum(cols[:, iq:iq + 1] * state, axis=0, keepdims=True)

    @pl.when(pl.program_id(0) > 0)
    def _():
        s_ref[...] = jnp.zeros_like(s_ref)


def _sample_gla(q, k, eg, v, s0_all, layer, s_all):
    n = q.shape[0]
    tiles = n // BT_GLA
    first = s_all is None
    n_slabs = s0_all.shape[0] - layer if first else 1
    tile = lambda s, i: jnp.where(s == 0, i, tiles - 1)
    row = lambda s, i: (tile(s, i), 0)
    blk = (None, BT_GLA, B_HEADS, B_HK, B_HV)
    in_specs = [
        pl.BlockSpec((BT_GLA, B_KEY_DIM), row), pl.BlockSpec((BT_GLA, B_KEY_DIM), row),
        pl.BlockSpec((BT_GLA, B_KEY_DIM), row), pl.BlockSpec((BT_GLA, D_MODEL), row),
        pl.BlockSpec(blk, lambda s, i: (layer, tile(s, i), 0, 0, 0)),
    ]
    args = [q, k, eg, v, s0_all]
    aliases = {}
    if not first:
        in_specs.append(pl.BlockSpec(memory_space=pl.ANY))
        args.append(s_all)
        aliases = {len(args) - 1: 0}
    return pl.pallas_call(
        _sample_gla_kernel,
        grid=(n_slabs, tiles),
        in_specs=in_specs,
        out_specs=[pl.BlockSpec(blk, lambda s, i: (layer + s, i, 0, 0, 0)),
                   pl.BlockSpec((BT_GLA, D_MODEL), row)],
        out_shape=[jax.ShapeDtypeStruct(s0_all.shape, F32),
                   jax.ShapeDtypeStruct((n, D_MODEL), F32)],
        input_output_aliases=aliases,
        compiler_params=_params(("arbitrary", "arbitrary")),
        name="sample_gla",
    )(*args)


PACK_ROWS = C_HEADS * C_HD // 128


def _pack_heads(a):
    lead = a.shape[:-2]
    a = a.reshape(*lead, C_HEADS, C_HD // 128, 128)
    return jnp.swapaxes(a, -2, -3).reshape(*lead, PACK_ROWS, 128)


def _unpack_heads(a):
    lead = a.shape[:-2]
    a = a.reshape(*lead, C_HD // 128, C_HEADS, 128)
    return jnp.swapaxes(a, -2, -3).reshape(*lead, C_HEADS, C_HD)


def _sample_att_kernel(cq_ref, mk_ref, mv_ref, o_ref):
    for j in range(BT_ATT):
        q = cq_ref[j]
        r = jnp.sum(mk_ref[j] * q[None], axis=-1, keepdims=True)
        s = r + pltpu.roll(r, C_HEADS, 1)
        e = jnp.exp(s - jnp.max(s, axis=0, keepdims=True))
        den = jnp.sum(e, axis=0)
        o_ref[j] = jnp.sum(e * mv_ref[j], axis=0) / den


def _sample_att(cq_packed, mk_all, mv_all, layer):
    n = cq_packed.shape[0]
    row = lambda i: (i, 0, 0)
    cache = lambda i: (layer, i, 0, 0, 0)
    return pl.pallas_call(
        _sample_att_kernel,
        grid=(n // BT_ATT,),
        in_specs=[
            pl.BlockSpec((BT_ATT, PACK_ROWS, 128), row),
            pl.BlockSpec((None, BT_ATT, N_MEM, PACK_ROWS, 128), cache),
            pl.BlockSpec((None, BT_ATT, N_MEM, PACK_ROWS, 128), cache),
        ],
        out_specs=pl.BlockSpec((BT_ATT, PACK_ROWS, 128), row),
        out_shape=jax.ShapeDtypeStruct((n, PACK_ROWS, 128), F32),
        compiler_params=_params(("parallel",)),
        name="sample_att",
    )(cq_packed, mk_all, mv_all)


def _sample_back_kernel(x_ref, h_ref, oa_ref, og_ref, szb_ref, oc_ref, szc_ref, gnf_ref,
                        wg_ref, bg_ref, wbr_ref, wout_ref, gf_ref, xo_ref, *, final):
    h = h_ref[...]
    out_b = (_head_rms(og_ref[...], gnf_ref[...]) * szb_ref[...]).astype(BF16)
    out_c = (oc_ref[...] * szc_ref[...]).astype(BF16)
    merged = None
    for i, br in enumerate((oa_ref[...], out_b, out_c)):
        cs = slice(i * D_MODEL, (i + 1) * D_MODEL)
        gate = _sigmoid(_dot(h, wg_ref[:, cs]) + bg_ref[:, cs])
        term = gate * _dot(br, wbr_ref[i])
        merged = term if merged is None else merged + term
    x_new = x_ref[...] + _dot(merged.astype(BF16), wout_ref[...])
    xo_ref[...] = _rms(x_new, gf_ref[...]) if final else x_new


def _sample_back(x, h, oa, og, szb, oc, szc, p, l):
    acts = (x, h, oa, og, szb, oc, szc)
    branch_w = pl.BlockSpec((N_BRANCH,) + p["wbr"].shape[1:], lambda i: (l, 0, 0),
                            pipeline_mode=pl.Buffered(1))
    return pl.pallas_call(
        functools.partial(_sample_back_kernel, final=l == DEPTH - 1),
        grid=(1,),
        in_specs=[_resident(a.shape) for a in acts] + [
            _layer(p["gnf"], l), _layer(p["wg"], l), _layer(p["bg"], l), branch_w,
            _layer(p["wout"], l), _resident(p["gf"].shape)],
        out_specs=pl.BlockSpec(x.shape, lambda i: (0, 0)),
        out_shape=jax.ShapeDtypeStruct(x.shape, F32),
        compiler_params=_params(("arbitrary",)),
        name="sample_back",
    )(*acts, p["gnf"], p["wg"], p["bg"], p["wbr"], p["wout"], p["gf"])


def kernel(x_prompt, x_sample, state_gla, cache_mem_k, cache_mem_v, mem_prompt, g_norm, w_in, b_gate, w_s, b_s, ln_v_g, ln_v_b, w_g_up, b_g, gn_g, g_mem, w_mem_kv, w_branch, w_out, g_final):
    batch, seq, _ = x_prompt.shape
    dec_batch = x_sample.shape[0]
    rows = lambda a: a.reshape(DEPTH, 1, -1)

    wa, wb, wc, wg = _split_w_in(jnp.swapaxes(w_in, 1, 2))
    p = {
        "wa": wa, "wb": wb, "wc": wc, "wg": wg,
        "wbr": _cast_bf16(w_branch.reshape(DEPTH * N_BRANCH, D_MODEL, D_MODEL)),
        "wout": _cast_bf16(w_out),
        "wgup": jnp.pad(w_g_up, ((0, 0), (0, GATE_RANK_PAD - GATE_RANK), (0, 0))).astype(BF16),
        "gn": rows(g_norm), "lng": rows(ln_v_g), "lnb": rows(ln_v_b), "bgu": rows(b_g),
        "bg": rows(b_gate), "gnf": rows(jnp.tile(gn_g, (1, B_HEADS))),
        "ws": w_s,
        "bs_full": jnp.repeat(jnp.swapaxes(b_s, 1, 2), A_HEAD_DIM, axis=2),
        "ws0": rows(jnp.repeat(w_s[:, :, 0, 0], A_HEAD_DIM, axis=1)),
        "bs0": rows(jnp.repeat(b_s[:, :, 0], A_HEAD_DIM, axis=1)),
        "gf": g_final.reshape(1, D_MODEL),
    }

    mk_out, mv_out, mk_all, mv_all = _memkv(
        mem_prompt, g_mem.reshape(DEPTH, 1, D_MODEL), _cast_bf16(w_mem_kv))

    xp = x_prompt.reshape(batch * seq, D_MODEL)
    xs = x_sample.reshape(dec_batch, D_MODEL)
    cache_k = _pack_heads(cache_mem_k)
    cache_v = _pack_heads(cache_mem_v)
    gla_p, v_s = [], []
    gla_s = None
    for l in range(DEPTH):
        h, m = _prompt_a(xp, p, l)
        m, st = _prompt_b(h, m, p, l, batch)
        xp = _prompt_c(xp, h, m, mk_all, mv_all, p, l, batch)
        gla_p.append(jnp.swapaxes(st, -1, -2))

        (hs_, vn, oa, q, k, eg, v, szb, cq, szc) = _sample_front(xs, p, l)
        gla_s, og = _sample_gla(q, k, eg, v, state_gla, l, gla_s)
        cq_packed = _pack_heads(cq.reshape(dec_batch, C_HEADS, C_HD))
        oc = _unpack_heads(_sample_att(cq_packed, cache_k, cache_v, l))
        xs = _sample_back(xs, hs_, oa, og, szb, oc.reshape(dec_batch, D_MODEL), szc, p, l)
        v_s.append(vn)

    y_prompt = xp.reshape(batch, seq, D_MODEL)
    y_sample = xs.reshape(dec_batch, 1, D_MODEL)
    return (y_prompt, y_sample, jnp.stack(gla_p), mk_out, mv_out, gla_s,
            jnp.stack(v_s).reshape(DEPTH, dec_batch, 1, A_HEADS, A_HEAD_DIM))
```

```python
import functools

import jax
import jax.numpy as jnp
from jax import lax
from jax.experimental import pallas as pl
from jax.experimental.pallas import tpu as pltpu

F32 = jnp.float32
BF16 = jnp.bfloat16

D_MODEL = 1024
DEPTH = 4
CHUNK_A = 128
A_HEADS = 8
A_HEAD_DIM = 128
B_HEADS = 4
B_KEY_DIM = 512
B_HK = 128
B_HV = 256
GATE_RANK = 16
GATE_RANK_PAD = 128
INV_GATE_NORMALIZER = 1.0 / 16.0
CHUNK_B = 64
N_MEM = 256
C_HEADS = 4
C_HD = 256
N_BRANCH = 3
EPS = 1e-6

VMEM_LIMIT_BYTES = 56 * 1024 * 1024
TM = 512
BT_GLA = 8
BT_ATT = 4


def _dot(a, b):
    return jnp.dot(a, b, preferred_element_type=F32)


def _dot_nt(a, b):
    return lax.dot_general(a, b, (((1,), (1,)), ((), ())), preferred_element_type=F32)


def _dot_tn(a, b):
    return lax.dot_general(a, b, (((0,), (0,)), ((), ())), preferred_element_type=F32)


def _sigmoid(x):
    return 1.0 / (1.0 + jnp.exp(-x))


def _silu(x):
    return x * _sigmoid(x)


def _gelu(x):
    c = 0.7978845608028654
    return x * (0.5 * (1.0 + jnp.tanh(c * (x + 0.044715 * (x * x * x)))))


def _log_sigmoid(x):
    return jnp.minimum(x, 0.0) - jnp.log1p(jnp.exp(-jnp.abs(x)))


def _rms(x, g):
    return x * lax.rsqrt(jnp.mean(x * x, axis=-1, keepdims=True) + EPS) * g


def _head_rms(o, g_full):
    parts = []
    for hd in range(B_HEADS):
        oh = o[:, hd * B_HV:(hd + 1) * B_HV]
        parts.append(oh * lax.rsqrt(jnp.mean(oh * oh, axis=-1, keepdims=True) + EPS))
    return jnp.concatenate(parts, axis=1) * g_full


def _layernorm(v, g, b):
    mu = jnp.mean(v, axis=-1, keepdims=True)
    vc = v - mu
    var = jnp.mean(vc * vc, axis=-1, keepdims=True)
    return vc * lax.rsqrt(var + EPS) * g + b


def _resident(shape):
    nd = len(shape)
    return pl.BlockSpec(shape, lambda *_: (0,) * nd, pipeline_mode=pl.Buffered(1))


def _layer(arr, layer, cols=None, col_block=0):
    shape = arr.shape[1:] if cols is None else arr.shape[1:-1] + (cols,)
    idx = (layer,) + (0,) * (len(shape) - 1) + (col_block,)
    return pl.BlockSpec((None,) + shape, lambda *_: idx, pipeline_mode=pl.Buffered(1))


def _params(semantics=None):
    return pltpu.CompilerParams(dimension_semantics=semantics,
                                vmem_limit_bytes=VMEM_LIMIT_BYTES)


O_BQ = 3 * D_MODEL
O_GD = O_BQ + 2 * B_KEY_DIM + D_MODEL
O_BZ = O_GD + GATE_RANK
O_CQ = O_BZ + D_MODEL
O_GL = O_CQ + 2 * D_MODEL
IN_COLS = O_GL + 3 * D_MODEL
WB_COLS = 2 * B_KEY_DIM + 2 * D_MODEL + GATE_RANK_PAD
TR_PREP = 128


def _split_w_in_kernel(wt_ref, wa_ref, wb_ref, wc_ref, wg_ref):
    def seg(lo, hi):
        return wt_ref[lo:hi, :].T.astype(BF16)

    wa_ref[...] = seg(0, O_BQ)
    wb_ref[:, :O_GD - O_BQ] = seg(O_BQ, O_GD)
    wb_ref[:, O_GD - O_BQ:O_GD - O_BQ + D_MODEL] = seg(O_BZ, O_CQ)
    gd = wt_ref[O_GD:O_GD + GATE_RANK_PAD, :].T
    lane = lax.broadcasted_iota(jnp.int32, gd.shape, 1)
    wb_ref[:, O_GD - O_BQ + D_MODEL:] = jnp.where(lane < GATE_RANK, gd, 0.0).astype(BF16)
    wc_ref[...] = seg(O_CQ, O_GL)
    wg_ref[...] = seg(O_GL, IN_COLS)


def _split_w_in(w_in_t):
    depth, _, rows = w_in_t.shape
    widths = (O_BQ, WB_COLS, O_GL - O_CQ, IN_COLS - O_GL)
    return pl.pallas_call(
        _split_w_in_kernel,
        grid=(depth, rows // TR_PREP),
        in_specs=[pl.BlockSpec((None, IN_COLS, TR_PREP), lambda l, i: (l, 0, i))],
        out_specs=[pl.BlockSpec((None, TR_PREP, n), lambda l, i: (l, i, 0)) for n in widths],
        out_shape=[jax.ShapeDtypeStruct((depth, rows, n), BF16) for n in widths],
        compiler_params=_params(("parallel", "parallel")),
        name="split_w_in",
    )(w_in_t)


def _cast_kernel(w_ref, o_ref):
    o_ref[...] = w_ref[...].astype(BF16)


def _cast_bf16(w):
    n, rows, cols = w.shape
    spec = pl.BlockSpec((None, rows, cols), lambda i: (i, 0, 0))
    return pl.pallas_call(
        _cast_kernel,
        grid=(n,),
        in_specs=[spec],
        out_specs=spec,
        out_shape=jax.ShapeDtypeStruct(w.shape, BF16),
        compiler_params=_params(("parallel",)),
        name="cast_bf16",
    )(w)


def _memkv_kernel(mem_ref, g_ref, w_ref, mk_ref, mv_ref, mk16_ref, mv16_ref):
    hn = _rms(mem_ref[...], g_ref[...]).astype(BF16)
    mk = _dot(hn, w_ref[:, :D_MODEL])
    mv = _dot(hn, w_ref[:, D_MODEL:])
    for hd in range(C_HEADS):
        hs = slice(hd * C_HD, (hd + 1) * C_HD)
        mk_ref[:, hd, :] = mk[:, hs]
        mv_ref[:, hd, :] = mv[:, hs]
    mk16_ref[...] = mk.astype(BF16)
    mv16_ref[...] = mv.astype(BF16)


def _memkv(mem, g_mem, w_kv):
    batch = mem.shape[0]
    out5 = jax.ShapeDtypeStruct((DEPTH, batch, N_MEM, C_HEADS, C_HD), F32)
    out16 = jax.ShapeDtypeStruct((DEPTH, batch * N_MEM, D_MODEL), BF16)
    spec5 = pl.BlockSpec((None, None, N_MEM, C_HEADS, C_HD), lambda l, b: (l, b, 0, 0, 0))
    spec16 = pl.BlockSpec((None, N_MEM, D_MODEL), lambda l, b: (l, b, 0))
    return pl.pallas_call(
        _memkv_kernel,
        grid=(DEPTH, batch),
        in_specs=[
            pl.BlockSpec((None, N_MEM, D_MODEL), lambda l, b: (b, 0, 0)),
            pl.BlockSpec((None, 1, D_MODEL), lambda l, b: (l, 0, 0)),
            pl.BlockSpec((None, D_MODEL, 2 * D_MODEL), lambda l, b: (l, 0, 0)),
        ],
        out_specs=[spec5, spec5, spec16, spec16],
        out_shape=[out5, out5, out16, out16],
        compiler_params=_params(("arbitrary", "arbitrary")),
        name="memkv",
    )(mem, g_mem, w_kv)


def _prompt_a_kernel(x_ref, gn_ref, wa_ref, lng_ref, lnb_ref, ws_ref, bs_ref,
                     wg_ref, bg_ref, wbr_ref, h_ref, m_ref, s_scr):
    n_chunks = TM // CHUNK_A
    h = _rms(x_ref[...], gn_ref[...]).astype(BF16)
    h_ref[...] = h
    v_pre = _dot(h, wa_ref[:, D_MODEL:2 * D_MODEL])
    u_pre = _dot(h, wa_ref[:, :D_MODEL])
    z = _dot(h, wa_ref[:, 2 * D_MODEL:])
    gate_logit = _dot(h, wg_ref[...]) + bg_ref[...]
    v = _gelu(v_pre)
    vn = _layernorm(v, lng_ref[...], lnb_ref[...]).astype(BF16)
    row = lax.broadcasted_iota(jnp.int32, (CHUNK_A, CHUNK_A), 0)
    col = lax.broadcasted_iota(jnp.int32, (CHUNK_A, CHUNK_A), 1)
    causal = row >= col
    for hd in range(A_HEADS):
        cs = slice(hd * A_HEAD_DIM, (hd + 1) * A_HEAD_DIM)
        w = jnp.where(causal, ws_ref[hd], 0.0).astype(BF16)
        vh = jnp.concatenate(
            [vn[c * CHUNK_A:(c + 1) * CHUNK_A, cs] for c in range(n_chunks)], axis=1)
        mixed = _dot(w, vh)
        for c in range(n_chunks):
            s_scr[c * CHUNK_A:(c + 1) * CHUNK_A, cs] = mixed[:, c * CHUNK_A:(c + 1) * CHUNK_A]
    bias = jnp.concatenate([bs_ref[...]] * n_chunks, axis=0)
    out_a = (_gelu(u_pre) * (s_scr[...] + bias) * _silu(z)).astype(BF16)
    m_ref[...] = _sigmoid(gate_logit) * _dot(out_a, wbr_ref[...])


def _prompt_a(x, p, l):
    n = x.shape[0]
    tile = lambda i: (i, 0)
    args = (x, p["gn"], p["wa"], p["lng"], p["lnb"], p["ws"], p["bs_full"],
            p["wg"], p["bg"], p["wbr"])
    return pl.pallas_call(
        _prompt_a_kernel,
        grid=(n // TM,),
        in_specs=[
            pl.BlockSpec((TM, D_MODEL), tile),
            _layer(p["gn"], l), _layer(p["wa"], l), _layer(p["lng"], l),
            _layer(p["lnb"], l), _layer(p["ws"], l), _layer(p["bs_full"], l),
            _layer(p["wg"], l, D_MODEL, 0), _layer(p["bg"], l, D_MODEL, 0),
            _layer(p["wbr"], N_BRANCH * l + 0),
        ],
        out_specs=[pl.BlockSpec((TM, D_MODEL), tile), pl.BlockSpec((TM, D_MODEL), tile)],
        out_shape=[jax.ShapeDtypeStruct((n, D_MODEL), BF16),
                   jax.ShapeDtypeStruct((n, D_MODEL), F32)],
        scratch_shapes=[pltpu.VMEM((TM, D_MODEL), F32)],
        compiler_params=_params(("parallel",)),
        name="prompt_a",
    )(*args)


def _prompt_b_kernel(h_ref, ma_ref, wb_ref, wgup_ref, bgu_ref, gnf_ref,
                     wg_ref, bg_ref, wbr_ref, m_ref, st_ref, o_scr):
    @pl.when(pl.program_id(1) == 0)
    def _():
        st_ref[...] = jnp.zeros_like(st_ref)

    h = h_ref[...]
    q = _dot(h, wb_ref[:, 0:B_KEY_DIM]) * (B_HK ** -0.5)
    k = _dot(h, wb_ref[:, B_KEY_DIM:2 * B_KEY_DIM])
    v = _dot(h, wb_ref[:, 2 * B_KEY_DIM:2 * B_KEY_DIM + D_MODEL])
    gd = _dot(h, wb_ref[:, 2 * B_KEY_DIM + 2 * D_MODEL:]).astype(BF16)
    logit = _dot(gd, wgup_ref[...]) + bgu_ref[...]
    z_off = 2 * B_KEY_DIM + D_MODEL
    z = _dot(h, wb_ref[:, z_off:z_off + D_MODEL])
    gate_logit = _dot(h, wg_ref[...]) + bg_ref[...]

    n_chunks = TM // CHUNK_B
    chunk_rows = [slice(c * CHUNK_B, (c + 1) * CHUNK_B) for c in range(n_chunks)]
    row = lax.broadcasted_iota(jnp.int32, (CHUNK_B, CHUNK_B), 0)
    col = lax.broadcasted_iota(jnp.int32, (CHUNK_B, CHUNK_B), 1)
    causal = row >= col
    tri = jnp.where(causal, 1.0, 0.0).astype(BF16)

    g = _log_sigmoid(logit) * INV_GATE_NORMALIZER
    g_hi = g.astype(BF16)
    g_lo = (g - g_hi.astype(F32)).astype(BF16)
    bcum = jnp.concatenate(
        [_dot(tri, g_hi[rs]) + _dot(tri, g_lo[rs]) for rs in chunk_rows], axis=0)
    bcum3 = bcum.reshape(n_chunks, CHUNK_B, B_KEY_DIM)
    b_last3 = bcum3[:, CHUNK_B - 1:CHUNK_B, :]
    q_dec = (q * jnp.exp(bcum)).astype(BF16)
    k_dec = (k * jnp.exp(-bcum)).astype(BF16)
    k_tail = (k.reshape(bcum3.shape) * jnp.exp(b_last3 - bcum3)).astype(BF16).reshape(k.shape)
    decay3 = jnp.exp(b_last3)
    vb = v.astype(BF16)

    scores, kv_t = {}, {}
    for c, rs in enumerate(chunk_rows):
        for hd in range(B_HEADS):
            ks = slice(hd * B_HK, (hd + 1) * B_HK)
            vs = slice(hd * B_HV, (hd + 1) * B_HV)
            scores[c, hd] = jnp.where(
                causal, _dot_nt(q_dec[rs, ks], k_dec[rs, ks]), 0.0).astype(BF16)
            kv_t[c, hd] = _dot_tn(vb[rs, vs], k_tail[rs, ks])
    silu_z = _silu(z)
    gate = _sigmoid(gate_logit)

    for hd in range(B_HEADS):
        ks = slice(hd * B_HK, (hd + 1) * B_HK)
        vs = slice(hd * B_HV, (hd + 1) * B_HV)
        state_t = st_ref[hd]
        for c, rs in enumerate(chunk_rows):
            o_scr[rs, vs] = (_dot(scores[c, hd], vb[rs, vs])
                             + _dot_nt(q_dec[rs, ks], state_t.astype(BF16)))
            state_t = state_t * decay3[c][:, ks] + kv_t[c, hd]
        st_ref[hd] = state_t

    out_b = (_head_rms(o_scr[...], gnf_ref[...]) * silu_z).astype(BF16)
    m_ref[...] = ma_ref[...] + gate * _dot(out_b, wbr_ref[...])


def _prompt_b(h, ma, p, l, batch):
    n = h.shape[0]
    steps = n // batch // TM
    tile = lambda b, t: (b * steps + t, 0)
    args = (h, ma, p["wb"], p["wgup"], p["bgu"], p["gnf"], p["wg"], p["bg"], p["wbr"])
    return pl.pallas_call(
        _prompt_b_kernel,
        grid=(batch, steps),
        in_specs=[
            pl.BlockSpec((TM, D_MODEL), tile), pl.BlockSpec((TM, D_MODEL), tile),
            _layer(p["wb"], l), _layer(p["wgup"], l), _layer(p["bgu"], l),
            _layer(p["gnf"], l),
            _layer(p["wg"], l, D_MODEL, 1), _layer(p["bg"], l, D_MODEL, 1),
            _layer(p["wbr"], N_BRANCH * l + 1),
        ],
        out_specs=[
            pl.BlockSpec((TM, D_MODEL), tile),
            pl.BlockSpec((None, B_HEADS, B_HV, B_HK), lambda b, t: (b, 0, 0, 0)),
        ],
        out_shape=[jax.ShapeDtypeStruct((n, D_MODEL), F32),
                   jax.ShapeDtypeStruct((batch, B_HEADS, B_HV, B_HK), F32)],
        scratch_shapes=[pltpu.VMEM((TM, D_MODEL), F32)],
        compiler_params=_params(("parallel", "arbitrary")),
        name="prompt_b",
    )(*args)


def _prompt_c_kernel(x_ref, h_ref, mab_ref, mk_ref, mv_ref, wc_ref,
                     wg_ref, bg_ref, wbr_ref, wout_ref, gf_ref, xo_ref, *, final):
    h = h_ref[...]
    q = _dot(h, wc_ref[:, :D_MODEL]).astype(BF16)
    z = _dot(h, wc_ref[:, D_MODEL:])
    gate_logit = _dot(h, wg_ref[...]) + bg_ref[...]
    mk = mk_ref[...]
    mv = mv_ref[...]
    parts = []
    for hd in range(C_HEADS):
        hs = slice(hd * C_HD, (hd + 1) * C_HD)
        s = _dot_nt(q[:, hs], mk[:, hs]) * (C_HD ** -0.5)
        e = jnp.exp(s - jnp.max(s, axis=-1, keepdims=True))
        p = e / jnp.sum(e, axis=-1, keepdims=True)
        parts.append(_dot(p.astype(BF16), mv[:, hs]))
    out_c = (jnp.concatenate(parts, axis=1) * _silu(z)).astype(BF16)
    merged = mab_ref[...] + _sigmoid(gate_logit) * _dot(out_c, wbr_ref[...])
    x_new = x_ref[...] + _dot(merged.astype(BF16), wout_ref[...])
    xo_ref[...] = _rms(x_new, gf_ref[...]) if final else x_new


def _prompt_c(x, h, mab, mk_all, mv_all, p, l, batch):
    n = x.shape[0]
    steps = n // batch // TM
    tile = lambda b, t: (b * steps + t, 0)
    mem = lambda b, t: (l, b, 0)
    args = (x, h, mab, mk_all, mv_all, p["wc"], p["wg"], p["bg"], p["wbr"], p["wout"], p["gf"])
    return pl.pallas_call(
        functools.partial(_prompt_c_kernel, final=l == DEPTH - 1),
        grid=(batch, steps),
        in_specs=[
            pl.BlockSpec((TM, D_MODEL), tile), pl.BlockSpec((TM, D_MODEL), tile),
            pl.BlockSpec((TM, D_MODEL), tile),
            pl.BlockSpec((None, N_MEM, D_MODEL), mem), pl.BlockSpec((None, N_MEM, D_MODEL), mem),
            _layer(p["wc"], l),
            _layer(p["wg"], l, D_MODEL, 2), _layer(p["bg"], l, D_MODEL, 2),
            _layer(p["wbr"], N_BRANCH * l + 2), _layer(p["wout"], l),
            _resident(p["gf"].shape),
        ],
        out_specs=pl.BlockSpec((TM, D_MODEL), tile),
        out_shape=jax.ShapeDtypeStruct((n, D_MODEL), F32),
        compiler_params=_params(("parallel", "arbitrary")),
        name="prompt_c",
    )(*args)


def _sample_front_kernel(x_ref, gn_ref, wa_ref, wb_ref, wc_ref, lng_ref, lnb_ref,
                         ws0_ref, bs0_ref, wgup_ref, bgu_ref,
                         h_ref, vn_ref, oa_ref, q_ref, k_ref, eg_ref, v_ref,
                         szb_ref, cq_ref, szc_ref):
    h = _rms(x_ref[...], gn_ref[...]).astype(BF16)
    h_ref[...] = h
    u = _gelu(_dot(h, wa_ref[:, :D_MODEL]))
    vn = _layernorm(_gelu(_dot(h, wa_ref[:, D_MODEL:2 * D_MODEL])), lng_ref[...], lnb_ref[...])
    vn_ref[...] = vn
    z = _dot(h, wa_ref[:, 2 * D_MODEL:])
    oa_ref[...] = (u * (vn * ws0_ref[...] + bs0_ref[...]) * _silu(z)).astype(BF16)
    q_ref[...] = _dot(h, wb_ref[:, 0:B_KEY_DIM]) * (B_HK ** -0.5)
    k_ref[...] = _dot(h, wb_ref[:, B_KEY_DIM:2 * B_KEY_DIM])
    v_ref[...] = _dot(h, wb_ref[:, 2 * B_KEY_DIM:2 * B_KEY_DIM + D_MODEL])
    szb_ref[...] = _silu(_dot(h, wb_ref[:, 2 * B_KEY_DIM + D_MODEL:2 * B_KEY_DIM + 2 * D_MODEL]))
    gd = _dot(h, wb_ref[:, 2 * B_KEY_DIM + 2 * D_MODEL:]).astype(BF16)
    g = _log_sigmoid(_dot(gd, wgup_ref[...]) + bgu_ref[...]) * INV_GATE_NORMALIZER
    eg_ref[...] = jnp.exp(g)
    cq_ref[...] = _dot(h, wc_ref[:, :D_MODEL]) * (C_HD ** -0.5)
    szc_ref[...] = _silu(_dot(h, wc_ref[:, D_MODEL:]))


def _sample_front(x, p, l):
    n = x.shape[0]
    wide = jax.ShapeDtypeStruct((n, D_MODEL), F32)
    half = jax.ShapeDtypeStruct((n, B_KEY_DIM), F32)
    wide16 = jax.ShapeDtypeStruct((n, D_MODEL), BF16)
    out_shape = [wide16, wide, wide16, half, half, half, wide, wide, wide, wide]
    names = ("gn", "wa", "wb", "wc", "lng", "lnb", "ws0", "bs0", "wgup", "bgu")
    return pl.pallas_call(
        _sample_front_kernel,
        grid=(1,),
        in_specs=[_resident(x.shape)] + [_layer(p[k], l) for k in names],
        out_specs=[pl.BlockSpec(s.shape, lambda i: (0, 0)) for s in out_shape],
        out_shape=out_shape,
        compiler_params=_params(("arbitrary",)),
        name="sample_front",
    )(x, *(p[k] for k in names))


def _sample_gla_kernel(q_ref, k_ref, eg_ref, v_ref, s0_ref, *rest):
    s_ref, o_ref = rest[-2:]

    @pl.when(pl.program_id(0) == 0)
    def _():
        pieces = []
        for ref in (q_ref, k_ref, eg_ref):
            for hd in range(B_HEADS):
                pieces.append(ref[:, hd * B_HK:(hd + 1) * B_HK])
        used = len(pieces) * BT_GLA
        pieces.append(jnp.zeros((B_HK - used, B_HK), F32))
        cols = jnp.concatenate(pieces, axis=0).T
        for j in range(BT_GLA):
            for hd in range(B_HEADS):
                vs = slice(hd * B_HV, (hd + 1) * B_HV)
                iq = hd * BT_GLA + j
                ik = (B_HEADS + hd) * BT_GLA + j
                ie = (2 * B_HEADS + hd) * BT_GLA + j
                state = (cols[:, ie:ie + 1] * s0_ref[j, hd]
                         + cols[:, ik:ik + 1] * v_ref[j:j + 1, vs])
                s_ref[j, hd] = state
                o_ref[j:j + 1, vs] = jnp.sum(cols[:, iq:iq + 1] * state, axis=0, keepdims=True)

    @pl.when(pl.program_id(0) > 0)
    def _():
        s_ref[...] = jnp.zeros_like(s_ref)


def _sample_gla(q, k, eg, v, s0_all, layer, s_all):
    n = q.shape[0]
    tiles = n // BT_GLA
    first = s_all is None
    n_slabs = s0_all.shape[0] - layer if first else 1
    tile = lambda s, i: jnp.where(s == 0, i, tiles - 1)
    row = lambda s, i: (tile(s, i), 0)
    blk = (None, BT_GLA, B_HEADS, B_HK, B_HV)
    in_specs = [
        pl.BlockSpec((BT_GLA, B_KEY_DIM), row), pl.BlockSpec((BT_GLA, B_KEY_DIM), row),
        pl.BlockSpec((BT_GLA, B_KEY_DIM), row), pl.BlockSpec((BT_GLA, D_MODEL), row),
        pl.BlockSpec(blk, lambda s, i: (layer, tile(s, i), 0, 0, 0)),
    ]
    args = [q, k, eg, v, s0_all]
    aliases = {}
    if not first:
        in_specs.append(pl.BlockSpec(memory_space=pl.ANY))
        args.append(s_all)
        aliases = {len(args) - 1: 0}
    return pl.pallas_call(
        _sample_gla_kernel,
        grid=(n_slabs, tiles),
        in_specs=in_specs,
        out_specs=[pl.BlockSpec(blk, lambda s, i: (layer + s, i, 0, 0, 0)),
                   pl.BlockSpec((BT_GLA, D_MODEL), row)],
        out_shape=[jax.ShapeDtypeStruct(s0_all.shape, F32),
                   jax.ShapeDtypeStruct((n, D_MODEL), F32)],
        input_output_aliases=aliases,
        compiler_params=_params(("arbitrary", "arbitrary")),
        name="sample_gla",
    )(*args)


PACK_ROWS = C_HEADS * C_HD // 128


def _pack_heads(a):
    lead = a.shape[:-2]
    a = a.reshape(*lead, C_HEADS, C_HD // 128, 128)
    return jnp.swapaxes(a, -2, -3).reshape(*lead, PACK_ROWS, 128)


def _unpack_heads(a):
    lead = a.shape[:-2]
    a = a.reshape(*lead, C_HD // 128, C_HEADS, 128)
    return jnp.swapaxes(a, -2, -3).reshape(*lead, C_HEADS, C_HD)


def _sample_att_kernel(cq_ref, mk_ref, mv_ref, o_ref):
    for j in range(BT_ATT):
        q = cq_ref[j]
        r = jnp.sum(mk_ref[j] * q[None], axis=-1, keepdims=True)
        s = r + pltpu.roll(r, C_HEADS, 1)
        e = jnp.exp(s - jnp.max(s, axis=0, keepdims=True))
        den = jnp.sum(e, axis=0)
        o_ref[j] = jnp.sum(e * mv_ref[j], axis=0) / den


def _sample_att(cq_packed, mk_all, mv_all, layer):
    n = cq_packed.shape[0]
    row = lambda i: (i, 0, 0)
    cache = lambda i: (layer, i, 0, 0, 0)
    return pl.pallas_call(
        _sample_att_kernel,
        grid=(n // BT_ATT,),
        in_specs=[
            pl.BlockSpec((BT_ATT, PACK_ROWS, 128), row),
            pl.BlockSpec((None, BT_ATT, N_MEM, PACK_ROWS, 128), cache),
            pl.BlockSpec((None, BT_ATT, N_MEM, PACK_ROWS, 128), cache),
        ],
        out_specs=pl.BlockSpec((BT_ATT, PACK_ROWS, 128), row),
        out_shape=jax.ShapeDtypeStruct((n, PACK_ROWS, 128), F32),
        compiler_params=_params(("parallel",)),
        name="sample_att",
    )(cq_packed, mk_all, mv_all)


def _sample_back_kernel(x_ref, h_ref, oa_ref, og_ref, szb_ref, oc_ref, szc_ref, gnf_ref,
                        wg_ref, bg_ref, wbr_ref, wout_ref, gf_ref, xo_ref, *, final):
    h = h_ref[...]
    out_b = (_head_rms(og_ref[...], gnf_ref[...]) * szb_ref[...]).astype(BF16)
    out_c = (oc_ref[...] * szc_ref[...]).astype(BF16)
    merged = None
    for i, br in enumerate((oa_ref[...], out_b, out_c)):
        cs = slice(i * D_MODEL, (i + 1) * D_MODEL)
        gate = _sigmoid(_dot(h, wg_ref[:, cs]) + bg_ref[:, cs])
        term = gate * _dot(br, wbr_ref[i])
        merged = term if merged is None else merged + term
    x_new = x_ref[...] + _dot(merged.astype(BF16), wout_ref[...])
    xo_ref[...] = _rms(x_new, gf_ref[...]) if final else x_new


def _sample_back(x, h, oa, og, szb, oc, szc, p, l):
    acts = (x, h, oa, og, szb, oc, szc)
    branch_w = pl.BlockSpec((N_BRANCH,) + p["wbr"].shape[1:], lambda i: (l, 0, 0),
                            pipeline_mode=pl.Buffered(1))
    return pl.pallas_call(
        functools.partial(_sample_back_kernel, final=l == DEPTH - 1),
        grid=(1,),
        in_specs=[_resident(a.shape) for a in acts] + [
            _layer(p["gnf"], l), _layer(p["wg"], l), _layer(p["bg"], l), branch_w,
            _layer(p["wout"], l), _resident(p["gf"].shape)],
        out_specs=pl.BlockSpec(x.shape, lambda i: (0, 0)),
        out_shape=jax.ShapeDtypeStruct(x.shape, F32),
        compiler_params=_params(("arbitrary",)),
        name="sample_back",
    )(*acts, p["gnf"], p["wg"], p["bg"], p["wbr"], p["wout"], p["gf"])


def kernel(x_prompt, x_sample, state_gla, cache_mem_k, cache_mem_v, mem_prompt, g_norm, w_in, b_gate, w_s, b_s, ln_v_g, ln_v_b, w_g_up, b_g, gn_g, g_mem, w_mem_kv, w_branch, w_out, g_final):
    batch, seq, _ = x_prompt.shape
    dec_batch = x_sample.shape[0]
    rows = lambda a: a.reshape(DEPTH, 1, -1)

    wa, wb, wc, wg = _split_w_in(jnp.swapaxes(w_in, 1, 2))
    p = {
        "wa": wa, "wb": wb, "wc": wc, "wg": wg,
        "wbr": _cast_bf16(w_branch.reshape(DEPTH * N_BRANCH, D_MODEL, D_MODEL)),
        "wout": _cast_bf16(w_out),
        "wgup": jnp.pad(w_g_up, ((0, 0), (0, GATE_RANK_PAD - GATE_RANK), (0, 0))).astype(BF16),
        "gn": rows(g_norm), "lng": rows(ln_v_g), "lnb": rows(ln_v_b), "bgu": rows(b_g),
        "bg": rows(b_gate), "gnf": rows(jnp.tile(gn_g, (1, B_HEADS))),
        "ws": w_s,
        "bs_full": jnp.repeat(jnp.swapaxes(b_s, 1, 2), A_HEAD_DIM, axis=2),
        "ws0": rows(jnp.repeat(w_s[:, :, 0, 0], A_HEAD_DIM, axis=1)),
        "bs0": rows(jnp.repeat(b_s[:, :, 0], A_HEAD_DIM, axis=1)),
        "gf": g_final.reshape(1, D_MODEL),
    }

    mk_out, mv_out, mk_all, mv_all = _memkv(
        mem_prompt, g_mem.reshape(DEPTH, 1, D_MODEL), _cast_bf16(w_mem_kv))

    xp = x_prompt.reshape(batch * seq, D_MODEL)
    xs = x_sample.reshape(dec_batch, D_MODEL)
    cache_k = _pack_heads(cache_mem_k)
    cache_v = _pack_heads(cache_mem_v)
    gla_p, v_s = [], []
    gla_s = None
    for l in range(DEPTH):
        h, m = _prompt_a(xp, p, l)
        m, st = _prompt_b(h, m, p, l, batch)
        xp = _prompt_c(xp, h, m, mk_all, mv_all, p, l, batch)
        gla_p.append(jnp.swapaxes(st, -1, -2))

        (hs_, vn, oa, q, k, eg, v, szb, cq, szc) = _sample_front(xs, p, l)
        gla_s, og = _sample_gla(q, k, eg, v, state_gla, l, gla_s)
        cq_packed = _pack_heads(cq.reshape(dec_batch, C_HEADS, C_HD))
        oc = _unpack_heads(_sample_att(cq_packed, cache_k, cache_v, l))
        xs = _sample_back(xs, hs_, oa, og, szb, oc.reshape(dec_batch, D_MODEL), szc, p, l)
        v_s.append(vn)

    y_prompt = xp.reshape(batch, seq, D_MODEL)
    y_sample = xs.reshape(dec_batch, 1, D_MODEL)
    return (y_prompt, y_sample, jnp.stack(gla_p), mk_out, mv_out, gla_s,
            jnp.stack(v_s).reshape(DEPTH, dec_batch, 1, A_HEADS, A_HEAD_DIM))
```

```python
import functools

import jax
import jax.numpy as jnp
from jax import lax
from jax.experimental import pallas as pl
from jax.experimental.pallas import tpu as pltpu

F32 = jnp.float32
BF16 = jnp.bfloat16

D_MODEL = 1024
DEPTH = 4
CHUNK_A = 128
A_HEADS = 8
A_HEAD_DIM = 128
B_HEADS = 4
B_KEY_DIM = 512
B_HK = 128
B_HV = 256
GATE_RANK = 16
GATE_RANK_PAD = 128
INV_GATE_NORMALIZER = 1.0 / 16.0
CHUNK_B = 64
N_MEM = 256
C_HEADS = 4
C_HD = 256
N_BRANCH = 3
EPS = 1e-6

VMEM_LIMIT_BYTES = 56 * 1024 * 1024
TM = 512
BT_GLA = 8
BT_ATT = 4


def _dot(a, b):
    return jnp.dot(a, b, preferred_element_type=F32)


def _dot_nt(a, b):
    return lax.dot_general(a, b, (((1,), (1,)), ((), ())), preferred_element_type=F32)


def _dot_tn(a, b):
    return lax.dot_general(a, b, (((0,), (0,)), ((), ())), preferred_element_type=F32)


def _sigmoid(x):
    return 1.0 / (1.0 + jnp.exp(-x))


def _silu(x):
    return x * _sigmoid(x)


def _gelu(x):
    c = 0.7978845608028654
    return x * (0.5 * (1.0 + jnp.tanh(c * (x + 0.044715 * (x * x * x)))))


def _log_sigmoid(x):
    return jnp.minimum(x, 0.0) - jnp.log1p(jnp.exp(-jnp.abs(x)))


def _rms(x, g):
    return x * lax.rsqrt(jnp.mean(x * x, axis=-1, keepdims=True) + EPS) * g


def _head_rms(o, g_full):
    parts = []
    for hd in range(B_HEADS):
        oh = o[:, hd * B_HV:(hd + 1) * B_HV]
        parts.append(oh * lax.rsqrt(jnp.mean(oh * oh, axis=-1, keepdims=True) + EPS))
    return jnp.concatenate(parts, axis=1) * g_full


def _layernorm(v, g, b):
    mu = jnp.mean(v, axis=-1, keepdims=True)
    vc = v - mu
    var = jnp.mean(vc * vc, axis=-1, keepdims=True)
    return vc * lax.rsqrt(var + EPS) * g + b


def _resident(shape):
    nd = len(shape)
    return pl.BlockSpec(shape, lambda *_: (0,) * nd, pipeline_mode=pl.Buffered(1))


def _layer(arr, layer, cols=None, col_block=0):
    shape = arr.shape[1:] if cols is None else arr.shape[1:-1] + (cols,)
    idx = (layer,) + (0,) * (len(shape) - 1) + (col_block,)
    return pl.BlockSpec((None,) + shape, lambda *_: idx, pipeline_mode=pl.Buffered(1))


def _params(semantics=None):
    return pltpu.CompilerParams(dimension_semantics=semantics,
                                vmem_limit_bytes=VMEM_LIMIT_BYTES)


O_BQ = 3 * D_MODEL
O_GD = O_BQ + 2 * B_KEY_DIM + D_MODEL
O_BZ = O_GD + GATE_RANK
O_CQ = O_BZ + D_MODEL
O_GL = O_CQ + 2 * D_MODEL
IN_COLS = O_GL + 3 * D_MODEL
WB_COLS = 2 * B_KEY_DIM + 2 * D_MODEL + GATE_RANK_PAD
TR_PREP = 128


def _split_w_in_kernel(wt_ref, wa_ref, wb_ref, wc_ref, wg_ref):
    def seg(lo, hi):
        return wt_ref[lo:hi, :].T.astype(BF16)

    wa_ref[...] = seg(0, O_BQ)
    wb_ref[:, :O_GD - O_BQ] = seg(O_BQ, O_GD)
    wb_ref[:, O_GD - O_BQ:O_GD - O_BQ + D_MODEL] = seg(O_BZ, O_CQ)
    gd = wt_ref[O_GD:O_GD + GATE_RANK_PAD, :].T
    lane = lax.broadcasted_iota(jnp.int32, gd.shape, 1)
    wb_ref[:, O_GD - O_BQ + D_MODEL:] = jnp.where(lane < GATE_RANK, gd, 0.0).astype(BF16)
    wc_ref[...] = seg(O_CQ, O_GL)
    wg_ref[...] = seg(O_GL, IN_COLS)


def _split_w_in(w_in_t):
    depth, _, rows = w_in_t.shape
    widths = (O_BQ, WB_COLS, O_GL - O_CQ, IN_COLS - O_GL)
    return pl.pallas_call(
        _split_w_in_kernel,
        grid=(depth, rows // TR_PREP),
        in_specs=[pl.BlockSpec((None, IN_COLS, TR_PREP), lambda l, i: (l, 0, i))],
        out_specs=[pl.BlockSpec((None, TR_PREP, n), lambda l, i: (l, i, 0)) for n in widths],
        out_shape=[jax.ShapeDtypeStruct((depth, rows, n), BF16) for n in widths],
        compiler_params=_params(("parallel", "parallel")),
        name="split_w_in",
    )(w_in_t)


def _cast_kernel(w_ref, o_ref):
    o_ref[...] = w_ref[...].astype(BF16)


def _cast_bf16(w):
    n, rows, cols = w.shape
    spec = pl.BlockSpec((None, rows, cols), lambda i: (i, 0, 0))
    return pl.pallas_call(
        _cast_kernel,
        grid=(n,),
        in_specs=[spec],
        out_specs=spec,
        out_shape=jax.ShapeDtypeStruct(w.shape, BF16),
        compiler_params=_params(("parallel",)),
        name="cast_bf16",
    )(w)


def _memkv_kernel(mem_ref, g_ref, w_ref, mk_ref, mv_ref, mk16_ref, mv16_ref):
    hn = _rms(mem_ref[...], g_ref[...]).astype(BF16)
    mk = _dot(hn, w_ref[:, :D_MODEL])
    mv = _dot(hn, w_ref[:, D_MODEL:])
    for hd in range(C_HEADS):
        hs = slice(hd * C_HD, (hd + 1) * C_HD)
        mk_ref[:, hd, :] = mk[:, hs]
        mv_ref[:, hd, :] = mv[:, hs]
    mk16_ref[...] = mk.astype(BF16)
    mv16_ref[...] = mv.astype(BF16)


def _memkv(mem, g_mem, w_kv):
    batch = mem.shape[0]
    out5 = jax.ShapeDtypeStruct((DEPTH, batch, N_MEM, C_HEADS, C_HD), F32)
    out16 = jax.ShapeDtypeStruct((DEPTH, batch * N_MEM, D_MODEL), BF16)
    spec5 = pl.BlockSpec((None, None, N_MEM, C_HEADS, C_HD), lambda l, b: (l, b, 0, 0, 0))
    spec16 = pl.BlockSpec((None, N_MEM, D_MODEL), lambda l, b: (l, b, 0))
    return pl.pallas_call(
        _memkv_kernel,
        grid=(DEPTH, batch),
        in_specs=[
            pl.BlockSpec((None, N_MEM, D_MODEL), lambda l, b: (b, 0, 0)),
            pl.BlockSpec((None, 1, D_MODEL), lambda l, b: (l, 0, 0)),
            pl.BlockSpec((None, D_MODEL, 2 * D_MODEL), lambda l, b: (l, 0, 0)),
        ],
        out_specs=[spec5, spec5, spec16, spec16],
        out_shape=[out5, out5, out16, out16],
        compiler_params=_params(("arbitrary", "arbitrary")),
        name="memkv",
    )(mem, g_mem, w_kv)


def _prompt_a_kernel(x_ref, gn_ref, wa_ref, lng_ref, lnb_ref, ws_ref, bs_ref,
                     wg_ref, bg_ref, wbr_ref, h_ref, m_ref, s_scr):
    n_chunks = TM // CHUNK_A
    h = _rms(x_ref[...], gn_ref[...]).astype(BF16)
    h_ref[...] = h
    v = _gelu(_dot(h, wa_ref[:, D_MODEL:2 * D_MODEL]))
    vn = _layernorm(v, lng_ref[...], lnb_ref[...]).astype(BF16)
    row = lax.broadcasted_iota(jnp.int32, (CHUNK_A, CHUNK_A), 0)
    col = lax.broadcasted_iota(jnp.int32, (CHUNK_A, CHUNK_A), 1)
    causal = row >= col
    for hd in range(A_HEADS):
        cs = slice(hd * A_HEAD_DIM, (hd + 1) * A_HEAD_DIM)
        w = jnp.where(causal, ws_ref[hd], 0.0).astype(BF16)
        vh = jnp.concatenate(
            [vn[c * CHUNK_A:(c + 1) * CHUNK_A, cs] for c in range(n_chunks)], axis=1)
        mixed = _dot(w, vh)
        for c in range(n_chunks):
            s_scr[c * CHUNK_A:(c + 1) * CHUNK_A, cs] = mixed[:, c * CHUNK_A:(c + 1) * CHUNK_A]
    u = _gelu(_dot(h, wa_ref[:, :D_MODEL]))
    z = _dot(h, wa_ref[:, 2 * D_MODEL:])
    bias = jnp.concatenate([bs_ref[...]] * n_chunks, axis=0)
    out_a = (u * (s_scr[...] + bias) * _silu(z)).astype(BF16)
    gate = _sigmoid(_dot(h, wg_ref[...]) + bg_ref[...])
    m_ref[...] = gate * _dot(out_a, wbr_ref[...])


def _prompt_a(x, p, l):
    n = x.shape[0]
    tile = lambda i: (i, 0)
    args = (x, p["gn"], p["wa"], p["lng"], p["lnb"], p["ws"], p["bs_full"],
            p["wg"], p["bg"], p["wbr"])
    return pl.pallas_call(
        _prompt_a_kernel,
        grid=(n // TM,),
        in_specs=[
            pl.BlockSpec((TM, D_MODEL), tile),
            _layer(p["gn"], l), _layer(p["wa"], l), _layer(p["lng"], l),
            _layer(p["lnb"], l), _layer(p["ws"], l), _layer(p["bs_full"], l),
            _layer(p["wg"], l, D_MODEL, 0), _layer(p["bg"], l, D_MODEL, 0),
            _layer(p["wbr"], N_BRANCH * l + 0),
        ],
        out_specs=[pl.BlockSpec((TM, D_MODEL), tile), pl.BlockSpec((TM, D_MODEL), tile)],
        out_shape=[jax.ShapeDtypeStruct((n, D_MODEL), BF16),
                   jax.ShapeDtypeStruct((n, D_MODEL), F32)],
        scratch_shapes=[pltpu.VMEM((TM, D_MODEL), F32)],
        compiler_params=_params(("parallel",)),
        name="prompt_a",
    )(*args)


def _prompt_b_kernel(h_ref, ma_ref, wb_ref, wgup_ref, bgu_ref, gnf_ref,
                     wg_ref, bg_ref, wbr_ref, m_ref, st_ref, o_scr):
    @pl.when(pl.program_id(1) == 0)
    def _():
        st_ref[...] = jnp.zeros_like(st_ref)

    h = h_ref[...]
    q = _dot(h, wb_ref[:, 0:B_KEY_DIM]) * (B_HK ** -0.5)
    k = _dot(h, wb_ref[:, B_KEY_DIM:2 * B_KEY_DIM])
    v = _dot(h, wb_ref[:, 2 * B_KEY_DIM:2 * B_KEY_DIM + D_MODEL])
    gd = _dot(h, wb_ref[:, 2 * B_KEY_DIM + 2 * D_MODEL:]).astype(BF16)
    logit = _dot(gd, wgup_ref[...]) + bgu_ref[...]
    z_off = 2 * B_KEY_DIM + D_MODEL
    z = _dot(h, wb_ref[:, z_off:z_off + D_MODEL])
    gate_logit = _dot(h, wg_ref[...]) + bg_ref[...]

    n_chunks = TM // CHUNK_B
    chunk_rows = [slice(c * CHUNK_B, (c + 1) * CHUNK_B) for c in range(n_chunks)]
    row = lax.broadcasted_iota(jnp.int32, (CHUNK_B, CHUNK_B), 0)
    col = lax.broadcasted_iota(jnp.int32, (CHUNK_B, CHUNK_B), 1)
    causal = row >= col
    tri = jnp.where(causal, 1.0, 0.0).astype(BF16)

    g = _log_sigmoid(logit) * INV_GATE_NORMALIZER
    g_hi = g.astype(BF16)
    g_lo = (g - g_hi.astype(F32)).astype(BF16)
    bcum = jnp.concatenate(
        [_dot(tri, g_hi[rs]) + _dot(tri, g_lo[rs]) for rs in chunk_rows], axis=0)
    bcum3 = bcum.reshape(n_chunks, CHUNK_B, B_KEY_DIM)
    b_last3 = bcum3[:, CHUNK_B - 1:CHUNK_B, :]
    q_dec = (q * jnp.exp(bcum)).astype(BF16)
    k_dec = (k * jnp.exp(-bcum)).astype(BF16)
    k_tail = (k.reshape(bcum3.shape) * jnp.exp(b_last3 - bcum3)).astype(BF16).reshape(k.shape)
    decay3 = jnp.exp(b_last3)
    vb = v.astype(BF16)

    scores, kv_t = {}, {}
    for c, rs in enumerate(chunk_rows):
        for hd in range(B_HEADS):
            ks = slice(hd * B_HK, (hd + 1) * B_HK)
            vs = slice(hd * B_HV, (hd + 1) * B_HV)
            scores[c, hd] = jnp.where(
                causal, _dot_nt(q_dec[rs, ks], k_dec[rs, ks]), 0.0).astype(BF16)
            kv_t[c, hd] = _dot_tn(vb[rs, vs], k_tail[rs, ks])
    silu_z = _silu(z)
    gate = _sigmoid(gate_logit)

    for hd in range(B_HEADS):
        ks = slice(hd * B_HK, (hd + 1) * B_HK)
        vs = slice(hd * B_HV, (hd + 1) * B_HV)
        state_t = st_ref[hd]
        for c, rs in enumerate(chunk_rows):
            o_scr[rs, vs] = (_dot(scores[c, hd], vb[rs, vs])
                             + _dot_nt(q_dec[rs, ks], state_t.astype(BF16)))
            state_t = state_t * decay3[c][:, ks] + kv_t[c, hd]
        st_ref[hd] = state_t

    out_b = (_head_rms(o_scr[...], gnf_ref[...]) * silu_z).astype(BF16)
    m_ref[...] = ma_ref[...] + gate * _dot(out_b, wbr_ref[...])


def _prompt_b(h, ma, p, l, batch):
    n = h.shape[0]
    steps = n // batch // TM
    tile = lambda b, t: (b * steps + t, 0)
    args = (h, ma, p["wb"], p["wgup"], p["bgu"], p["gnf"], p["wg"], p["bg"], p["wbr"])
    return pl.pallas_call(
        _prompt_b_kernel,
        grid=(batch, steps),
        in_specs=[
            pl.BlockSpec((TM, D_MODEL), tile), pl.BlockSpec((TM, D_MODEL), tile),
            _layer(p["wb"], l), _layer(p["wgup"], l), _layer(p["bgu"], l),
            _layer(p["gnf"], l),
            _layer(p["wg"], l, D_MODEL, 1), _layer(p["bg"], l, D_MODEL, 1),
            _layer(p["wbr"], N_BRANCH * l + 1),
        ],
        out_specs=[
            pl.BlockSpec((TM, D_MODEL), tile),
            pl.BlockSpec((None, B_HEADS, B_HV, B_HK), lambda b, t: (b, 0, 0, 0)),
        ],
        out_shape=[jax.ShapeDtypeStruct((n, D_MODEL), F32),
                   jax.ShapeDtypeStruct((batch, B_HEADS, B_HV, B_HK), F32)],
        scratch_shapes=[pltpu.VMEM((TM, D_MODEL), F32)],
        compiler_params=_params(("parallel", "arbitrary")),
        name="prompt_b",
    )(*args)


def _prompt_c_kernel(x_ref, h_ref, mab_ref, mk_ref, mv_ref, wc_ref,
                     wg_ref, bg_ref, wbr_ref, wout_ref, gf_ref, cq_ref, ck_ref, cv_ref,
                     xo_ref, oc_ref, *, final):
    h = h_ref[...]
    q = _dot(h, wc_ref[:, :D_MODEL]).astype(BF16)
    z = _dot(h, wc_ref[:, D_MODEL:])
    gate_logit = _dot(h, wg_ref[...]) + bg_ref[...]
    mk = mk_ref[...]
    mv = mv_ref[...]
    parts = []
    for hd in range(C_HEADS):
        hs = slice(hd * C_HD, (hd + 1) * C_HD)
        s = _dot_nt(q[:, hs], mk[:, hs]) * (C_HD ** -0.5)
        e = jnp.exp(s - jnp.max(s, axis=-1, keepdims=True))
        p = e / jnp.sum(e, axis=-1, keepdims=True)
        parts.append(_dot(p.astype(BF16), mv[:, hs]))
    out_c = (jnp.concatenate(parts, axis=1) * _silu(z)).astype(BF16)
    merged = mab_ref[...] + _sigmoid(gate_logit) * _dot(out_c, wbr_ref[...])
    x_new = x_ref[...] + _dot(merged.astype(BF16), wout_ref[...])
    xo_ref[...] = _rms(x_new, gf_ref[...]) if final else x_new
    _sample_att_tile(cq_ref, ck_ref, cv_ref, oc_ref)


def _prompt_c(x, h, mab, mk_all, mv_all, cq_packed, cache_k, cache_v, p, l, batch):
    n = x.shape[0]
    steps = n // batch // TM
    dec_batch = cq_packed.shape[0]
    assert dec_batch == batch * steps * BT_ATT, "one sample batch tile per prompt grid step"
    tile = lambda b, t: (b * steps + t, 0)
    mem = lambda b, t: (l, b, 0)
    att_row = lambda b, t: (b * steps + t, 0, 0)
    att_cache = lambda b, t: (l, b * steps + t, 0, 0, 0)
    args = (x, h, mab, mk_all, mv_all, p["wc"], p["wg"], p["bg"], p["wbr"], p["wout"], p["gf"],
            cq_packed, cache_k, cache_v)
    return pl.pallas_call(
        functools.partial(_prompt_c_kernel, final=l == DEPTH - 1),
        grid=(batch, steps),
        in_specs=[
            pl.BlockSpec((TM, D_MODEL), tile), pl.BlockSpec((TM, D_MODEL), tile),
            pl.BlockSpec((TM, D_MODEL), tile),
            pl.BlockSpec((None, N_MEM, D_MODEL), mem), pl.BlockSpec((None, N_MEM, D_MODEL), mem),
            _layer(p["wc"], l),
            _layer(p["wg"], l, D_MODEL, 2), _layer(p["bg"], l, D_MODEL, 2),
            _layer(p["wbr"], N_BRANCH * l + 2), _layer(p["wout"], l),
            _resident(p["gf"].shape),
            pl.BlockSpec((BT_ATT, PACK_ROWS, 128), att_row),
            pl.BlockSpec((None, BT_ATT, N_MEM, PACK_ROWS, 128), att_cache),
            pl.BlockSpec((None, BT_ATT, N_MEM, PACK_ROWS, 128), att_cache),
        ],
        out_specs=[pl.BlockSpec((TM, D_MODEL), tile),
                   pl.BlockSpec((BT_ATT, PACK_ROWS, 128), att_row)],
        out_shape=[jax.ShapeDtypeStruct((n, D_MODEL), F32),
                   jax.ShapeDtypeStruct((dec_batch, PACK_ROWS, 128), F32)],
        compiler_params=_params(("parallel", "arbitrary")),
        name="prompt_c",
    )(*args)


def _sample_front_kernel(x_ref, gn_ref, wa_ref, wb_ref, wc_ref, lng_ref, lnb_ref,
                         ws0_ref, bs0_ref, wgup_ref, bgu_ref,
                         h_ref, vn_ref, oa_ref, q_ref, k_ref, eg_ref, v_ref,
                         szb_ref, cq_ref, szc_ref):
    h = _rms(x_ref[...], gn_ref[...]).astype(BF16)
    h_ref[...] = h
    u = _gelu(_dot(h, wa_ref[:, :D_MODEL]))
    vn = _layernorm(_gelu(_dot(h, wa_ref[:, D_MODEL:2 * D_MODEL])), lng_ref[...], lnb_ref[...])
    vn_ref[...] = vn
    z = _dot(h, wa_ref[:, 2 * D_MODEL:])
    oa_ref[...] = (u * (vn * ws0_ref[...] + bs0_ref[...]) * _silu(z)).astype(BF16)
    q_ref[...] = _dot(h, wb_ref[:, 0:B_KEY_DIM]) * (B_HK ** -0.5)
    k_ref[...] = _dot(h, wb_ref[:, B_KEY_DIM:2 * B_KEY_DIM])
    v_ref[...] = _dot(h, wb_ref[:, 2 * B_KEY_DIM:2 * B_KEY_DIM + D_MODEL])
    szb_ref[...] = _silu(_dot(h, wb_ref[:, 2 * B_KEY_DIM + D_MODEL:2 * B_KEY_DIM + 2 * D_MODEL]))
    gd = _dot(h, wb_ref[:, 2 * B_KEY_DIM + 2 * D_MODEL:]).astype(BF16)
    g = _log_sigmoid(_dot(gd, wgup_ref[...]) + bgu_ref[...]) * INV_GATE_NORMALIZER
    eg_ref[...] = jnp.exp(g)
    cq_ref[...] = _dot(h, wc_ref[:, :D_MODEL]) * (C_HD ** -0.5)
    szc_ref[...] = _silu(_dot(h, wc_ref[:, D_MODEL:]))


def _sample_front(x, p, l):
    n = x.shape[0]
    wide = jax.ShapeDtypeStruct((n, D_MODEL), F32)
    half = jax.ShapeDtypeStruct((n, B_KEY_DIM), F32)
    wide16 = jax.ShapeDtypeStruct((n, D_MODEL), BF16)
    out_shape = [wide16, wide, wide16, half, half, half, wide, wide, wide, wide]
    names = ("gn", "wa", "wb", "wc", "lng", "lnb", "ws0", "bs0", "wgup", "bgu")
    return pl.pallas_call(
        _sample_front_kernel,
        grid=(1,),
        in_specs=[_resident(x.shape)] + [_layer(p[k], l) for k in names],
        out_specs=[pl.BlockSpec(s.shape, lambda i: (0, 0)) for s in out_shape],
        out_shape=out_shape,
        compiler_params=_params(("arbitrary",)),
        name="sample_front",
    )(x, *(p[k] for k in names))


def _sample_gla_kernel(q_ref, k_ref, eg_ref, v_ref, s0_ref, *rest):
    s_ref, o_ref = rest[-2:]

    @pl.when(pl.program_id(0) == 0)
    def _():
        pieces = []
        for ref in (q_ref, k_ref, eg_ref):
            for hd in range(B_HEADS):
                pieces.append(ref[:, hd * B_HK:(hd + 1) * B_HK])
        used = len(pieces) * BT_GLA
        pieces.append(jnp.zeros((B_HK - used, B_HK), F32))
        cols = jnp.concatenate(pieces, axis=0).T
        for j in range(BT_GLA):
            for hd in range(B_HEADS):
                vs = slice(hd * B_HV, (hd + 1) * B_HV)
                iq = hd * BT_GLA + j
                ik = (B_HEADS + hd) * BT_GLA + j
                ie = (2 * B_HEADS + hd) * BT_GLA + j
                state = (cols[:, ie:ie + 1] * s0_ref[j, hd]
                         + cols[:, ik:ik + 1] * v_ref[j:j + 1, vs])
                s_ref[j, hd] = state
                o_ref[j:j + 1, vs] = jnp.sum(cols[:, iq:iq + 1] * state, axis=0, keepdims=True)

    @pl.when(pl.program_id(0) > 0)
    def _():
        s_ref[...] = jnp.zeros_like(s_ref)


def _sample_gla(q, k, eg, v, s0_all, layer, s_all):
    n = q.shape[0]
    tiles = n // BT_GLA
    first = s_all is None
    n_slabs = s0_all.shape[0] - layer if first else 1
    tile = lambda s, i: jnp.where(s == 0, i, tiles - 1)
    row = lambda s, i: (tile(s, i), 0)
    blk = (None, BT_GLA, B_HEADS, B_HK, B_HV)
    in_specs = [
        pl.BlockSpec((BT_GLA, B_KEY_DIM), row), pl.BlockSpec((BT_GLA, B_KEY_DIM), row),
        pl.BlockSpec((BT_GLA, B_KEY_DIM), row), pl.BlockSpec((BT_GLA, D_MODEL), row),
        pl.BlockSpec(blk, lambda s, i: (layer, tile(s, i), 0, 0, 0)),
    ]
    args = [q, k, eg, v, s0_all]
    aliases = {}
    if not first:
        in_specs.append(pl.BlockSpec(memory_space=pl.ANY))
        args.append(s_all)
        aliases = {len(args) - 1: 0}
    return pl.pallas_call(
        _sample_gla_kernel,
        grid=(n_slabs, tiles),
        in_specs=in_specs,
        out_specs=[pl.BlockSpec(blk, lambda s, i: (layer + s, i, 0, 0, 0)),
                   pl.BlockSpec((BT_GLA, D_MODEL), row)],
        out_shape=[jax.ShapeDtypeStruct(s0_all.shape, F32),
                   jax.ShapeDtypeStruct((n, D_MODEL), F32)],
        input_output_aliases=aliases,
        compiler_params=_params(("arbitrary", "arbitrary")),
        name="sample_gla",
    )(*args)


PACK_ROWS = C_HEADS * C_HD // 128


def _pack_heads(a):
    lead = a.shape[:-2]
    a = a.reshape(*lead, C_HEADS, C_HD // 128, 128)
    return jnp.swapaxes(a, -2, -3).reshape(*lead, PACK_ROWS, 128)


def _unpack_heads(a):
    lead = a.shape[:-2]
    a = a.reshape(*lead, C_HD // 128, C_HEADS, 128)
    return jnp.swapaxes(a, -2, -3).reshape(*lead, C_HEADS, C_HD)


def _sample_att_tile(cq_ref, mk_ref, mv_ref, o_ref):
    for j in range(BT_ATT):
        q = cq_ref[j]
        r = jnp.sum(mk_ref[j] * q[None], axis=-1, keepdims=True)
        s = r + pltpu.roll(r, C_HEADS, 1)
        e = jnp.exp(s - jnp.max(s, axis=0, keepdims=True))
        den = jnp.sum(e, axis=0)
        o_ref[j] = jnp.sum(e * mv_ref[j], axis=0) / den


def _sample_back_kernel(x_ref, h_ref, oa_ref, og_ref, szb_ref, oc_ref, szc_ref, gnf_ref,
                        wg_ref, bg_ref, wbr_ref, wout_ref, gf_ref, xo_ref, *, final):
    h = h_ref[...]
    out_b = (_head_rms(og_ref[...], gnf_ref[...]) * szb_ref[...]).astype(BF16)
    out_c = (oc_ref[...] * szc_ref[...]).astype(BF16)
    merged = None
    for i, br in enumerate((oa_ref[...], out_b, out_c)):
        cs = slice(i * D_MODEL, (i + 1) * D_MODEL)
        gate = _sigmoid(_dot(h, wg_ref[:, cs]) + bg_ref[:, cs])
        term = gate * _dot(br, wbr_ref[i])
        merged = term if merged is None else merged + term
    x_new = x_ref[...] + _dot(merged.astype(BF16), wout_ref[...])
    xo_ref[...] = _rms(x_new, gf_ref[...]) if final else x_new


def _sample_back(x, h, oa, og, szb, oc, szc, p, l):
    acts = (x, h, oa, og, szb, oc, szc)
    branch_w = pl.BlockSpec((N_BRANCH,) + p["wbr"].shape[1:], lambda i: (l, 0, 0),
                            pipeline_mode=pl.Buffered(1))
    return pl.pallas_call(
        functools.partial(_sample_back_kernel, final=l == DEPTH - 1),
        grid=(1,),
        in_specs=[_resident(a.shape) for a in acts] + [
            _layer(p["gnf"], l), _layer(p["wg"], l), _layer(p["bg"], l), branch_w,
            _layer(p["wout"], l), _resident(p["gf"].shape)],
        out_specs=pl.BlockSpec(x.shape, lambda i: (0, 0)),
        out_shape=jax.ShapeDtypeStruct(x.shape, F32),
        compiler_params=_params(("arbitrary",)),
        name="sample_back",
    )(*acts, p["gnf"], p["wg"], p["bg"], p["wbr"], p["wout"], p["gf"])


def kernel(x_prompt, x_sample, state_gla, cache_mem_k, cache_mem_v, mem_prompt, g_norm, w_in, b_gate, w_s, b_s, ln_v_g, ln_v_b, w_g_up, b_g, gn_g, g_mem, w_mem_kv, w_branch, w_out, g_final):
    batch, seq, _ = x_prompt.shape
    dec_batch = x_sample.shape[0]
    rows = lambda a: a.reshape(DEPTH, 1, -1)

    wa, wb, wc, wg = _split_w_in(jnp.swapaxes(w_in, 1, 2))
    p = {
        "wa": wa, "wb": wb, "wc": wc, "wg": wg,
        "wbr": _cast_bf16(w_branch.reshape(DEPTH * N_BRANCH, D_MODEL, D_MODEL)),
        "wout": _cast_bf16(w_out),
        "wgup": jnp.pad(w_g_up, ((0, 0), (0, GATE_RANK_PAD - GATE_RANK), (0, 0))).astype(BF16),
        "gn": rows(g_norm), "lng": rows(ln_v_g), "lnb": rows(ln_v_b), "bgu": rows(b_g),
        "bg": rows(b_gate), "gnf": rows(jnp.tile(gn_g, (1, B_HEADS))),
        "ws": w_s,
        "bs_full": jnp.repeat(jnp.swapaxes(b_s, 1, 2), A_HEAD_DIM, axis=2),
        "ws0": rows(jnp.repeat(w_s[:, :, 0, 0], A_HEAD_DIM, axis=1)),
        "bs0": rows(jnp.repeat(b_s[:, :, 0], A_HEAD_DIM, axis=1)),
        "gf": g_final.reshape(1, D_MODEL),
    }

    mk_out, mv_out, mk_all, mv_all = _memkv(
        mem_prompt, g_mem.reshape(DEPTH, 1, D_MODEL), _cast_bf16(w_mem_kv))

    xp = x_prompt.reshape(batch * seq, D_MODEL)
    xs = x_sample.reshape(dec_batch, D_MODEL)
    cache_k = _pack_heads(cache_mem_k)
    cache_v = _pack_heads(cache_mem_v)
    gla_p, v_s = [], []
    gla_s = None
    for l in range(DEPTH):
        (hs_, vn, oa, q, k, eg, v, szb, cq, szc) = _sample_front(xs, p, l)
        cq_packed = _pack_heads(cq.reshape(dec_batch, C_HEADS, C_HD))

        h, m = _prompt_a(xp, p, l)
        m, st = _prompt_b(h, m, p, l, batch)
        xp, oc = _prompt_c(xp, h, m, mk_all, mv_all, cq_packed, cache_k, cache_v, p, l, batch)
        gla_p.append(jnp.swapaxes(st, -1, -2))

        gla_s, og = _sample_gla(q, k, eg, v, state_gla, l, gla_s)
        oc = _unpack_heads(oc).reshape(dec_batch, D_MODEL)
        xs = _sample_back(xs, hs_, oa, og, szb, oc, szc, p, l)
        v_s.append(vn)

    y_prompt = xp.reshape(batch, seq, D_MODEL)
    y_sample = xs.reshape(dec_batch, 1, D_MODEL)
    return (y_prompt, y_sample, jnp.stack(gla_p), mk_out, mv_out, gla_s,
            jnp.stack(v_s).reshape(DEPTH, dec_batch, 1, A_HEADS, A_HEAD_DIM))
```

```python
import functools

import jax
import jax.numpy as jnp
from jax import lax
from jax.experimental import pallas as pl
from jax.experimental.pallas import tpu as pltpu

F32 = jnp.float32
BF16 = jnp.bfloat16

D_MODEL = 1024
DEPTH = 4
CHUNK_A = 128
A_HEADS = 8
A_HEAD_DIM = 128
B_HEADS = 4
B_KEY_DIM = 512
B_HK = 128
B_HV = 256
GATE_RANK = 16
GATE_RANK_PAD = 128
INV_GATE_NORMALIZER = 1.0 / 16.0
CHUNK_B = 64
N_MEM = 256
C_HEADS = 4
C_HD = 256
N_BRANCH = 3
EPS = 1e-6

VMEM_LIMIT_BYTES = 56 * 1024 * 1024
TM = 512
BT_GLA = 4
BT_ATT = 4


def _dot(a, b):
    return jnp.dot(a, b, preferred_element_type=F32)


def _dot_nt(a, b):
    return lax.dot_general(a, b, (((1,), (1,)), ((), ())), preferred_element_type=F32)


def _dot_tn(a, b):
    return lax.dot_general(a, b, (((0,), (0,)), ((), ())), preferred_element_type=F32)


def _sigmoid(x):
    return 1.0 / (1.0 + jnp.exp(-x))


def _silu(x):
    return x * _sigmoid(x)


def _gelu(x):
    c = 0.7978845608028654
    return x * (0.5 * (1.0 + jnp.tanh(c * (x + 0.044715 * (x * x * x)))))


def _log_sigmoid(x):
    return jnp.minimum(x, 0.0) - jnp.log1p(jnp.exp(-jnp.abs(x)))


def _rms(x, g):
    return x * lax.rsqrt(jnp.mean(x * x, axis=-1, keepdims=True) + EPS) * g


def _head_rms(o, g_full):
    parts = []
    for hd in range(B_HEADS):
        oh = o[:, hd * B_HV:(hd + 1) * B_HV]
        parts.append(oh * lax.rsqrt(jnp.mean(oh * oh, axis=-1, keepdims=True) + EPS))
    return jnp.concatenate(parts, axis=1) * g_full


def _layernorm(v, g, b):
    mu = jnp.mean(v, axis=-1, keepdims=True)
    vc = v - mu
    var = jnp.mean(vc * vc, axis=-1, keepdims=True)
    return vc * lax.rsqrt(var + EPS) * g + b


def _resident(shape):
    nd = len(shape)
    return pl.BlockSpec(shape, lambda *_: (0,) * nd, pipeline_mode=pl.Buffered(1))


def _layer(arr, layer, cols=None, col_block=0):
    shape = arr.shape[1:] if cols is None else arr.shape[1:-1] + (cols,)
    idx = (layer,) + (0,) * (len(shape) - 1) + (col_block,)
    return pl.BlockSpec((None,) + shape, lambda *_: idx, pipeline_mode=pl.Buffered(1))


def _params(semantics=None):
    return pltpu.CompilerParams(dimension_semantics=semantics,
                                vmem_limit_bytes=VMEM_LIMIT_BYTES)


O_BQ = 3 * D_MODEL
O_GD = O_BQ + 2 * B_KEY_DIM + D_MODEL
O_BZ = O_GD + GATE_RANK
O_CQ = O_BZ + D_MODEL
O_GL = O_CQ + 2 * D_MODEL
IN_COLS = O_GL + 3 * D_MODEL
WB_COLS = 2 * B_KEY_DIM + 2 * D_MODEL + GATE_RANK_PAD
TR_PREP = 128


def _split_w_in_kernel(wt_ref, wa_ref, wb_ref, wc_ref, wg_ref):
    def seg(lo, hi):
        return wt_ref[lo:hi, :].T.astype(BF16)

    wa_ref[...] = seg(0, O_BQ)
    wb_ref[:, :O_GD - O_BQ] = seg(O_BQ, O_GD)
    wb_ref[:, O_GD - O_BQ:O_GD - O_BQ + D_MODEL] = seg(O_BZ, O_CQ)
    gd = wt_ref[O_GD:O_GD + GATE_RANK_PAD, :].T
    lane = lax.broadcasted_iota(jnp.int32, gd.shape, 1)
    wb_ref[:, O_GD - O_BQ + D_MODEL:] = jnp.where(lane < GATE_RANK, gd, 0.0).astype(BF16)
    wc_ref[...] = seg(O_CQ, O_GL)
    wg_ref[...] = seg(O_GL, IN_COLS)


def _split_w_in(w_in_t):
    depth, _, rows = w_in_t.shape
    widths = (O_BQ, WB_COLS, O_GL - O_CQ, IN_COLS - O_GL)
    return pl.pallas_call(
        _split_w_in_kernel,
        grid=(depth, rows // TR_PREP),
        in_specs=[pl.BlockSpec((None, IN_COLS, TR_PREP), lambda l, i: (l, 0, i))],
        out_specs=[pl.BlockSpec((None, TR_PREP, n), lambda l, i: (l, i, 0)) for n in widths],
        out_shape=[jax.ShapeDtypeStruct((depth, rows, n), BF16) for n in widths],
        compiler_params=_params(("parallel", "parallel")),
        name="split_w_in",
    )(w_in_t)


def _cast_kernel(w_ref, o_ref):
    o_ref[...] = w_ref[...].astype(BF16)


def _cast_bf16(w):
    n, rows, cols = w.shape
    spec = pl.BlockSpec((None, rows, cols), lambda i: (i, 0, 0))
    return pl.pallas_call(
        _cast_kernel,
        grid=(n,),
        in_specs=[spec],
        out_specs=spec,
        out_shape=jax.ShapeDtypeStruct(w.shape, BF16),
        compiler_params=_params(("parallel",)),
        name="cast_bf16",
    )(w)


def _memkv_kernel(mem_ref, g_ref, w_ref, mk_ref, mv_ref, mk16_ref, mv16_ref):
    hn = _rms(mem_ref[...], g_ref[...]).astype(BF16)
    mk = _dot(hn, w_ref[:, :D_MODEL])
    mv = _dot(hn, w_ref[:, D_MODEL:])
    for hd in range(C_HEADS):
        hs = slice(hd * C_HD, (hd + 1) * C_HD)
        mk_ref[:, hd, :] = mk[:, hs]
        mv_ref[:, hd, :] = mv[:, hs]
    mk16_ref[...] = mk.astype(BF16)
    mv16_ref[...] = mv.astype(BF16)


def _memkv(mem, g_mem, w_kv):
    batch = mem.shape[0]
    out5 = jax.ShapeDtypeStruct((DEPTH, batch, N_MEM, C_HEADS, C_HD), F32)
    out16 = jax.ShapeDtypeStruct((DEPTH, batch * N_MEM, D_MODEL), BF16)
    spec5 = pl.BlockSpec((None, None, N_MEM, C_HEADS, C_HD), lambda l, b: (l, b, 0, 0, 0))
    spec16 = pl.BlockSpec((None, N_MEM, D_MODEL), lambda l, b: (l, b, 0))
    return pl.pallas_call(
        _memkv_kernel,
        grid=(DEPTH, batch),
        in_specs=[
            pl.BlockSpec((None, N_MEM, D_MODEL), lambda l, b: (b, 0, 0)),
            pl.BlockSpec((None, 1, D_MODEL), lambda l, b: (l, 0, 0)),
            pl.BlockSpec((None, D_MODEL, 2 * D_MODEL), lambda l, b: (l, 0, 0)),
        ],
        out_specs=[spec5, spec5, spec16, spec16],
        out_shape=[out5, out5, out16, out16],
        compiler_params=_params(("arbitrary", "arbitrary")),
        name="memkv",
    )(mem, g_mem, w_kv)


def _prompt_a_kernel(x_ref, gn_ref, wa_ref, lng_ref, lnb_ref, ws_ref, bs_ref,
                     wg_ref, bg_ref, wbr_ref, qkd_ref, sv_ref, s0_ref, *rest):
    h_ref, m_ref, sn_ref, so_ref, s_scr = rest[-5:]
    _sample_gla_tile(qkd_ref, sv_ref, s0_ref, sn_ref, so_ref)
    n_chunks = TM // CHUNK_A
    h = _rms(x_ref[...], gn_ref[...]).astype(BF16)
    h_ref[...] = h
    v = _gelu(_dot(h, wa_ref[:, D_MODEL:2 * D_MODEL]))
    vn = _layernorm(v, lng_ref[...], lnb_ref[...]).astype(BF16)
    row = lax.broadcasted_iota(jnp.int32, (CHUNK_A, CHUNK_A), 0)
    col = lax.broadcasted_iota(jnp.int32, (CHUNK_A, CHUNK_A), 1)
    causal = row >= col
    for hd in range(A_HEADS):
        cs = slice(hd * A_HEAD_DIM, (hd + 1) * A_HEAD_DIM)
        w = jnp.where(causal, ws_ref[hd], 0.0).astype(BF16)
        vh = jnp.concatenate(
            [vn[c * CHUNK_A:(c + 1) * CHUNK_A, cs] for c in range(n_chunks)], axis=1)
        mixed = _dot(w, vh)
        for c in range(n_chunks):
            s_scr[c * CHUNK_A:(c + 1) * CHUNK_A, cs] = mixed[:, c * CHUNK_A:(c + 1) * CHUNK_A]
    u = _gelu(_dot(h, wa_ref[:, :D_MODEL]))
    z = _dot(h, wa_ref[:, 2 * D_MODEL:])
    bias = jnp.concatenate([bs_ref[...]] * n_chunks, axis=0)
    out_a = (u * (s_scr[...] + bias) * _silu(z)).astype(BF16)
    gate = _sigmoid(_dot(h, wg_ref[...]) + bg_ref[...])
    m_ref[...] = gate * _dot(out_a, wbr_ref[...])


def _prompt_a(x, qkd, sv, s0_all, s_all, p, l):
    n = x.shape[0]
    steps = n // TM
    assert qkd.shape[0] == steps, "one sample batch tile per prompt grid step"
    tile = lambda i: (i, 0)
    first = s_all is None
    n_slabs = s0_all.shape[0] if first else 1
    assert not first or l == 0
    state_blk = (BT_GLA, B_HEADS, B_HK, B_HV)
    args = [x, p["gn"], p["wa"], p["lng"], p["lnb"], p["ws"], p["bs_full"],
            p["wg"], p["bg"], p["wbr"], qkd, sv, s0_all]
    in_specs = [
        pl.BlockSpec((TM, D_MODEL), tile),
        _layer(p["gn"], l), _layer(p["wa"], l), _layer(p["lng"], l),
        _layer(p["lnb"], l), _layer(p["ws"], l), _layer(p["bs_full"], l),
        _layer(p["wg"], l, D_MODEL, 0), _layer(p["bg"], l, D_MODEL, 0),
        _layer(p["wbr"], N_BRANCH * l + 0),
        pl.BlockSpec((None, GLA_ROWS, B_HK), lambda i: (i, 0, 0)),
        pl.BlockSpec((None, BT_GLA, D_MODEL), lambda i: (i, 0, 0)),
        pl.BlockSpec((None,) + state_blk, lambda i: (l, i, 0, 0, 0)),
    ]
    aliases = {}
    if not first:
        in_specs.append(pl.BlockSpec(memory_space=pl.ANY))
        args.append(s_all)
        aliases = {len(args) - 1: 2}
    return pl.pallas_call(
        _prompt_a_kernel,
        grid=(steps,),
        in_specs=in_specs,
        out_specs=[pl.BlockSpec((TM, D_MODEL), tile), pl.BlockSpec((TM, D_MODEL), tile),
                   pl.BlockSpec((n_slabs,) + state_blk, lambda i: (l, i, 0, 0, 0)),
                   pl.BlockSpec((None, BT_GLA, D_MODEL), lambda i: (i, 0, 0))],
        out_shape=[jax.ShapeDtypeStruct((n, D_MODEL), BF16),
                   jax.ShapeDtypeStruct((n, D_MODEL), F32),
                   jax.ShapeDtypeStruct(s0_all.shape, F32),
                   jax.ShapeDtypeStruct(sv.shape, F32)],
        scratch_shapes=[pltpu.VMEM((TM, D_MODEL), F32)],
        input_output_aliases=aliases,
        compiler_params=_params(("arbitrary",)),
        name="prompt_a",
    )(*args)


def _prompt_b_kernel(h_ref, ma_ref, wb_ref, wgup_ref, bgu_ref, gnf_ref,
                     wg_ref, bg_ref, wbr_ref, m_ref, st_ref, o_scr):
    @pl.when(pl.program_id(1) == 0)
    def _():
        st_ref[...] = jnp.zeros_like(st_ref)

    h = h_ref[...]
    q = _dot(h, wb_ref[:, 0:B_KEY_DIM]) * (B_HK ** -0.5)
    k = _dot(h, wb_ref[:, B_KEY_DIM:2 * B_KEY_DIM])
    v = _dot(h, wb_ref[:, 2 * B_KEY_DIM:2 * B_KEY_DIM + D_MODEL])
    gd = _dot(h, wb_ref[:, 2 * B_KEY_DIM + 2 * D_MODEL:]).astype(BF16)
    logit = _dot(gd, wgup_ref[...]) + bgu_ref[...]
    z_off = 2 * B_KEY_DIM + D_MODEL
    z = _dot(h, wb_ref[:, z_off:z_off + D_MODEL])
    gate_logit = _dot(h, wg_ref[...]) + bg_ref[...]

    n_chunks = TM // CHUNK_B
    chunk_rows = [slice(c * CHUNK_B, (c + 1) * CHUNK_B) for c in range(n_chunks)]
    row = lax.broadcasted_iota(jnp.int32, (CHUNK_B, CHUNK_B), 0)
    col = lax.broadcasted_iota(jnp.int32, (CHUNK_B, CHUNK_B), 1)
    causal = row >= col
    tri = jnp.where(causal, 1.0, 0.0).astype(BF16)

    g = _log_sigmoid(logit) * INV_GATE_NORMALIZER
    g_hi = g.astype(BF16)
    g_lo = (g - g_hi.astype(F32)).astype(BF16)
    bcum = jnp.concatenate(
        [_dot(tri, g_hi[rs]) + _dot(tri, g_lo[rs]) for rs in chunk_rows], axis=0)
    bcum3 = bcum.reshape(n_chunks, CHUNK_B, B_KEY_DIM)
    b_last3 = bcum3[:, CHUNK_B - 1:CHUNK_B, :]
    q_dec = (q * jnp.exp(bcum)).astype(BF16)
    k_dec = (k * jnp.exp(-bcum)).astype(BF16)
    k_tail = (k.reshape(bcum3.shape) * jnp.exp(b_last3 - bcum3)).astype(BF16).reshape(k.shape)
    decay3 = jnp.exp(b_last3)
    vb = v.astype(BF16)

    scores, kv_t = {}, {}
    for c, rs in enumerate(chunk_rows):
        for hd in range(B_HEADS):
            ks = slice(hd * B_HK, (hd + 1) * B_HK)
            vs = slice(hd * B_HV, (hd + 1) * B_HV)
            scores[c, hd] = jnp.where(
                causal, _dot_nt(q_dec[rs, ks], k_dec[rs, ks]), 0.0).astype(BF16)
            kv_t[c, hd] = _dot_tn(vb[rs, vs], k_tail[rs, ks])
    silu_z = _silu(z)
    gate = _sigmoid(gate_logit)

    for hd in range(B_HEADS):
        ks = slice(hd * B_HK, (hd + 1) * B_HK)
        vs = slice(hd * B_HV, (hd + 1) * B_HV)
        state_t = st_ref[hd]
        for c, rs in enumerate(chunk_rows):
            o_scr[rs, vs] = (_dot(scores[c, hd], vb[rs, vs])
                             + _dot_nt(q_dec[rs, ks], state_t.astype(BF16)))
            state_t = state_t * decay3[c][:, ks] + kv_t[c, hd]
        st_ref[hd] = state_t

    out_b = (_head_rms(o_scr[...], gnf_ref[...]) * silu_z).astype(BF16)
    m_ref[...] = ma_ref[...] + gate * _dot(out_b, wbr_ref[...])


def _prompt_b(h, ma, p, l, batch):
    n = h.shape[0]
    steps = n // batch // TM
    tile = lambda b, t: (b * steps + t, 0)
    args = (h, ma, p["wb"], p["wgup"], p["bgu"], p["gnf"], p["wg"], p["bg"], p["wbr"])
    return pl.pallas_call(
        _prompt_b_kernel,
        grid=(batch, steps),
        in_specs=[
            pl.BlockSpec((TM, D_MODEL), tile), pl.BlockSpec((TM, D_MODEL), tile),
            _layer(p["wb"], l), _layer(p["wgup"], l), _layer(p["bgu"], l),
            _layer(p["gnf"], l),
            _layer(p["wg"], l, D_MODEL, 1), _layer(p["bg"], l, D_MODEL, 1),
            _layer(p["wbr"], N_BRANCH * l + 1),
        ],
        out_specs=[
            pl.BlockSpec((TM, D_MODEL), tile),
            pl.BlockSpec((None, B_HEADS, B_HV, B_HK), lambda b, t: (b, 0, 0, 0)),
        ],
        out_shape=[jax.ShapeDtypeStruct((n, D_MODEL), F32),
                   jax.ShapeDtypeStruct((batch, B_HEADS, B_HV, B_HK), F32)],
        scratch_shapes=[pltpu.VMEM((TM, D_MODEL), F32)],
        compiler_params=_params(("parallel", "arbitrary")),
        name="prompt_b",
    )(*args)


def _prompt_c_kernel(x_ref, h_ref, mab_ref, mk_ref, mv_ref, wc_ref,
                     wg_ref, bg_ref, wbr_ref, wout_ref, gf_ref, cq_ref, ck_ref, cv_ref,
                     xo_ref, oc_ref, *, final):
    h = h_ref[...]
    q = _dot(h, wc_ref[:, :D_MODEL]).astype(BF16)
    z = _dot(h, wc_ref[:, D_MODEL:])
    gate_logit = _dot(h, wg_ref[...]) + bg_ref[...]
    mk = mk_ref[...]
    mv = mv_ref[...]
    parts = []
    for hd in range(C_HEADS):
        hs = slice(hd * C_HD, (hd + 1) * C_HD)
        s = _dot_nt(q[:, hs], mk[:, hs]) * (C_HD ** -0.5)
        e = jnp.exp(s - jnp.max(s, axis=-1, keepdims=True))
        p = e / jnp.sum(e, axis=-1, keepdims=True)
        parts.append(_dot(p.astype(BF16), mv[:, hs]))
    out_c = (jnp.concatenate(parts, axis=1) * _silu(z)).astype(BF16)
    merged = mab_ref[...] + _sigmoid(gate_logit) * _dot(out_c, wbr_ref[...])
    x_new = x_ref[...] + _dot(merged.astype(BF16), wout_ref[...])
    xo_ref[...] = _rms(x_new, gf_ref[...]) if final else x_new
    _sample_att_tile(cq_ref, ck_ref, cv_ref, oc_ref)


def _prompt_c(x, h, mab, mk_all, mv_all, cq_packed, cache_k, cache_v, p, l, batch):
    n = x.shape[0]
    steps = n // batch // TM
    dec_batch = cq_packed.shape[0]
    assert dec_batch == batch * steps * BT_ATT, "one sample batch tile per prompt grid step"
    tile = lambda b, t: (b * steps + t, 0)
    mem = lambda b, t: (l, b, 0)
    att_row = lambda b, t: (b * steps + t, 0, 0)
    att_cache = lambda b, t: (l, b * steps + t, 0, 0, 0)
    args = (x, h, mab, mk_all, mv_all, p["wc"], p["wg"], p["bg"], p["wbr"], p["wout"], p["gf"],
            cq_packed, cache_k, cache_v)
    return pl.pallas_call(
        functools.partial(_prompt_c_kernel, final=l == DEPTH - 1),
        grid=(batch, steps),
        in_specs=[
            pl.BlockSpec((TM, D_MODEL), tile), pl.BlockSpec((TM, D_MODEL), tile),
            pl.BlockSpec((TM, D_MODEL), tile),
            pl.BlockSpec((None, N_MEM, D_MODEL), mem), pl.BlockSpec((None, N_MEM, D_MODEL), mem),
            _layer(p["wc"], l),
            _layer(p["wg"], l, D_MODEL, 2), _layer(p["bg"], l, D_MODEL, 2),
            _layer(p["wbr"], N_BRANCH * l + 2), _layer(p["wout"], l),
            _resident(p["gf"].shape),
            pl.BlockSpec((BT_ATT, PACK_ROWS, 128), att_row),
            pl.BlockSpec((None, BT_ATT, N_MEM, PACK_ROWS, 128), att_cache),
            pl.BlockSpec((None, BT_ATT, N_MEM, PACK_ROWS, 128), att_cache),
        ],
        out_specs=[pl.BlockSpec((TM, D_MODEL), tile),
                   pl.BlockSpec((BT_ATT, PACK_ROWS, 128), att_row)],
        out_shape=[jax.ShapeDtypeStruct((n, D_MODEL), F32),
                   jax.ShapeDtypeStruct((dec_batch, PACK_ROWS, 128), F32)],
        compiler_params=_params(("parallel", "arbitrary")),
        name="prompt_c",
    )(*args)


def _sample_front_kernel(x_ref, gn_ref, wa_ref, wb_ref, wc_ref, lng_ref, lnb_ref,
                         ws0_ref, bs0_ref, wgup_ref, bgu_ref,
                         h_ref, vn_ref, oa_ref, q_ref, k_ref, eg_ref, v_ref,
                         szb_ref, cq_ref, szc_ref):
    h = _rms(x_ref[...], gn_ref[...]).astype(BF16)
    h_ref[...] = h
    u = _gelu(_dot(h, wa_ref[:, :D_MODEL]))
    vn = _layernorm(_gelu(_dot(h, wa_ref[:, D_MODEL:2 * D_MODEL])), lng_ref[...], lnb_ref[...])
    vn_ref[...] = vn
    z = _dot(h, wa_ref[:, 2 * D_MODEL:])
    oa_ref[...] = (u * (vn * ws0_ref[...] + bs0_ref[...]) * _silu(z)).astype(BF16)
    q_ref[...] = _dot(h, wb_ref[:, 0:B_KEY_DIM]) * (B_HK ** -0.5)
    k_ref[...] = _dot(h, wb_ref[:, B_KEY_DIM:2 * B_KEY_DIM])
    v_ref[...] = _dot(h, wb_ref[:, 2 * B_KEY_DIM:2 * B_KEY_DIM + D_MODEL])
    szb_ref[...] = _silu(_dot(h, wb_ref[:, 2 * B_KEY_DIM + D_MODEL:2 * B_KEY_DIM + 2 * D_MODEL]))
    gd = _dot(h, wb_ref[:, 2 * B_KEY_DIM + 2 * D_MODEL:]).astype(BF16)
    g = _log_sigmoid(_dot(gd, wgup_ref[...]) + bgu_ref[...]) * INV_GATE_NORMALIZER
    eg_ref[...] = jnp.exp(g)
    cq_ref[...] = _dot(h, wc_ref[:, :D_MODEL]) * (C_HD ** -0.5)
    szc_ref[...] = _silu(_dot(h, wc_ref[:, D_MODEL:]))


def _sample_front(x, p, l):
    n = x.shape[0]
    wide = jax.ShapeDtypeStruct((n, D_MODEL), F32)
    half = jax.ShapeDtypeStruct((n, B_KEY_DIM), F32)
    wide16 = jax.ShapeDtypeStruct((n, D_MODEL), BF16)
    out_shape = [wide16, wide, wide16, half, half, half, wide, wide, wide, wide]
    names = ("gn", "wa", "wb", "wc", "lng", "lnb", "ws0", "bs0", "wgup", "bgu")
    return pl.pallas_call(
        _sample_front_kernel,
        grid=(1,),
        in_specs=[_resident(x.shape)] + [_layer(p[k], l) for k in names],
        out_specs=[pl.BlockSpec(s.shape, lambda i: (0, 0)) for s in out_shape],
        out_shape=out_shape,
        compiler_params=_params(("arbitrary",)),
        name="sample_front",
    )(x, *(p[k] for k in names))


GLA_ROWS = 3 * B_HEADS * BT_GLA


def _sample_gla_tile(qkd_ref, v_ref, s0_ref, s_ref, o_ref):
    pad = jnp.zeros((B_HK - GLA_ROWS, B_HK), F32)
    cols = jnp.concatenate([qkd_ref[...], pad], axis=0).T
    for j in range(BT_GLA):
        for hd in range(B_HEADS):
            vs = slice(hd * B_HV, (hd + 1) * B_HV)
            iq = hd * BT_GLA + j
            ik = (B_HEADS + hd) * BT_GLA + j
            ie = (2 * B_HEADS + hd) * BT_GLA + j
            state = (cols[:, ie:ie + 1] * s0_ref[j, hd]
                     + cols[:, ik:ik + 1] * v_ref[j:j + 1, vs])
            s_ref[0, j, hd] = state
            o_ref[j:j + 1, vs] = jnp.sum(cols[:, iq:iq + 1] * state, axis=0, keepdims=True)
    if s_ref.shape[0] > 1:
        s_ref[1:] = jnp.zeros((s_ref.shape[0] - 1,) + s_ref.shape[1:], F32)


def _stack_gla_rows(q, k, decay):
    n = q.shape[0]
    a = jnp.stack([q, k, decay]).reshape(3, n // BT_GLA, BT_GLA, B_HEADS, B_HK)
    return a.transpose(1, 0, 3, 2, 4).reshape(n // BT_GLA, GLA_ROWS, B_HK)


PACK_ROWS = C_HEADS * C_HD // 128


def _pack_heads(a):
    lead = a.shape[:-2]
    a = a.reshape(*lead, C_HEADS, C_HD // 128, 128)
    return jnp.swapaxes(a, -2, -3).reshape(*lead, PACK_ROWS, 128)


def _unpack_heads(a):
    lead = a.shape[:-2]
    a = a.reshape(*lead, C_HD // 128, C_HEADS, 128)
    return jnp.swapaxes(a, -2, -3).reshape(*lead, C_HEADS, C_HD)


def _sample_att_tile(cq_ref, mk_ref, mv_ref, o_ref):
    for j in range(BT_ATT):
        q = cq_ref[j]
        r = jnp.sum(mk_ref[j] * q[None], axis=-1, keepdims=True)
        s = r + pltpu.roll(r, C_HEADS, 1)
        e = jnp.exp(s - jnp.max(s, axis=0, keepdims=True))
        den = jnp.sum(e, axis=0)
        o_ref[j] = jnp.sum(e * mv_ref[j], axis=0) / den


def _sample_back_kernel(x_ref, h_ref, oa_ref, og_ref, szb_ref, oc_ref, szc_ref, gnf_ref,
                        wg_ref, bg_ref, wbr_ref, wout_ref, gf_ref, xo_ref, *, final):
    h = h_ref[...]
    out_b = (_head_rms(og_ref[...], gnf_ref[...]) * szb_ref[...]).astype(BF16)
    out_c = (oc_ref[...] * szc_ref[...]).astype(BF16)
    merged = None
    for i, br in enumerate((oa_ref[...], out_b, out_c)):
        cs = slice(i * D_MODEL, (i + 1) * D_MODEL)
        gate = _sigmoid(_dot(h, wg_ref[:, cs]) + bg_ref[:, cs])
        term = gate * _dot(br, wbr_ref[i])
        merged = term if merged is None else merged + term
    x_new = x_ref[...] + _dot(merged.astype(BF16), wout_ref[...])
    xo_ref[...] = _rms(x_new, gf_ref[...]) if final else x_new


def _sample_back(x, h, oa, og, szb, oc, szc, p, l):
    acts = (x, h, oa, og, szb, oc, szc)
    branch_w = pl.BlockSpec((N_BRANCH,) + p["wbr"].shape[1:], lambda i: (l, 0, 0),
                            pipeline_mode=pl.Buffered(1))
    return pl.pallas_call(
        functools.partial(_sample_back_kernel, final=l == DEPTH - 1),
        grid=(1,),
        in_specs=[_resident(a.shape) for a in acts] + [
            _layer(p["gnf"], l), _layer(p["wg"], l), _layer(p["bg"], l), branch_w,
            _layer(p["wout"], l), _resident(p["gf"].shape)],
        out_specs=pl.BlockSpec(x.shape, lambda i: (0, 0)),
        out_shape=jax.ShapeDtypeStruct(x.shape, F32),
        compiler_params=_params(("arbitrary",)),
        name="sample_back",
    )(*acts, p["gnf"], p["wg"], p["bg"], p["wbr"], p["wout"], p["gf"])


def kernel(x_prompt, x_sample, state_gla, cache_mem_k, cache_mem_v, mem_prompt, g_norm, w_in, b_gate, w_s, b_s, ln_v_g, ln_v_b, w_g_up, b_g, gn_g, g_mem, w_mem_kv, w_branch, w_out, g_final):
    batch, seq, _ = x_prompt.shape
    dec_batch = x_sample.shape[0]
    rows = lambda a: a.reshape(DEPTH, 1, -1)

    wa, wb, wc, wg = _split_w_in(jnp.swapaxes(w_in, 1, 2))
    p = {
        "wa": wa, "wb": wb, "wc": wc, "wg": wg,
        "wbr": _cast_bf16(w_branch.reshape(DEPTH * N_BRANCH, D_MODEL, D_MODEL)),
        "wout": _cast_bf16(w_out),
        "wgup": jnp.pad(w_g_up, ((0, 0), (0, GATE_RANK_PAD - GATE_RANK), (0, 0))).astype(BF16),
        "gn": rows(g_norm), "lng": rows(ln_v_g), "lnb": rows(ln_v_b), "bgu": rows(b_g),
        "bg": rows(b_gate), "gnf": rows(jnp.tile(gn_g, (1, B_HEADS))),
        "ws": w_s,
        "bs_full": jnp.repeat(jnp.swapaxes(b_s, 1, 2), A_HEAD_DIM, axis=2),
        "ws0": rows(jnp.repeat(w_s[:, :, 0, 0], A_HEAD_DIM, axis=1)),
        "bs0": rows(jnp.repeat(b_s[:, :, 0], A_HEAD_DIM, axis=1)),
        "gf": g_final.reshape(1, D_MODEL),
    }

    mk_out, mv_out, mk_all, mv_all = _memkv(
        mem_prompt, g_mem.reshape(DEPTH, 1, D_MODEL), _cast_bf16(w_mem_kv))

    xp = x_prompt.reshape(batch * seq, D_MODEL)
    xs = x_sample.reshape(dec_batch, D_MODEL)
    cache_k = _pack_heads(cache_mem_k)
    cache_v = _pack_heads(cache_mem_v)
    gla_p, v_s = [], []
    gla_s = None
    for l in range(DEPTH):
        (hs_, vn, oa, q, k, eg, v, szb, cq, szc) = _sample_front(xs, p, l)
        cq_packed = _pack_heads(cq.reshape(dec_batch, C_HEADS, C_HD))

        qkd = _stack_gla_rows(q, k, eg)
        sv = v.reshape(dec_batch // BT_GLA, BT_GLA, D_MODEL)

        h, m, gla_s, og = _prompt_a(xp, qkd, sv, state_gla, gla_s, p, l)
        m, st = _prompt_b(h, m, p, l, batch)
        xp, oc = _prompt_c(xp, h, m, mk_all, mv_all, cq_packed, cache_k, cache_v, p, l, batch)
        gla_p.append(jnp.swapaxes(st, -1, -2))

        og = og.reshape(dec_batch, D_MODEL)
        oc = _unpack_heads(oc).reshape(dec_batch, D_MODEL)
        xs = _sample_back(xs, hs_, oa, og, szb, oc, szc, p, l)
        v_s.append(vn)

    y_prompt = xp.reshape(batch, seq, D_MODEL)
    y_sample = xs.reshape(dec_batch, 1, D_MODEL)
    return (y_prompt, y_sample, jnp.stack(gla_p), mk_out, mv_out, gla_s,
            jnp.stack(v_s).reshape(DEPTH, dec_batch, 1, A_HEADS, A_HEAD_DIM))
```

```python
import functools

import jax
import jax.numpy as jnp
from jax import lax
from jax.experimental import pallas as pl
from jax.experimental.pallas import tpu as pltpu

F32 = jnp.float32
BF16 = jnp.bfloat16

D_MODEL = 1024
DEPTH = 4
CHUNK_A = 128
A_HEADS = 8
A_HEAD_DIM = 128
B_HEADS = 4
B_KEY_DIM = 512
B_HK = 128
B_HV = 256
GATE_RANK = 16
GATE_RANK_PAD = 128
INV_GATE_NORMALIZER = 1.0 / 16.0
CHUNK_B = 64
N_MEM = 256
C_HEADS = 4
C_HD = 256
N_BRANCH = 3
EPS = 1e-6

VMEM_LIMIT_BYTES = 56 * 1024 * 1024
TM = 512
BT_GLA = 4
BT_ATT = 4


def _dot(a, b):
    return jnp.dot(a, b, preferred_element_type=F32)


def _dot_nt(a, b):
    return lax.dot_general(a, b, (((1,), (1,)), ((), ())), preferred_element_type=F32)


def _dot_tn(a, b):
    return lax.dot_general(a, b, (((0,), (0,)), ((), ())), preferred_element_type=F32)


def _sigmoid(x):
    return 1.0 / (1.0 + jnp.exp(-x))


def _silu(x):
    return x * _sigmoid(x)


def _gelu(x):
    c = 0.7978845608028654
    return x * (0.5 * (1.0 + jnp.tanh(c * (x + 0.044715 * (x * x * x)))))


def _log_sigmoid(x):
    return jnp.minimum(x, 0.0) - jnp.log1p(jnp.exp(-jnp.abs(x)))


def _rms(x, g):
    return x * lax.rsqrt(jnp.mean(x * x, axis=-1, keepdims=True) + EPS) * g


def _head_rms(o, g_full):
    parts = []
    for hd in range(B_HEADS):
        oh = o[:, hd * B_HV:(hd + 1) * B_HV]
        parts.append(oh * lax.rsqrt(jnp.mean(oh * oh, axis=-1, keepdims=True) + EPS))
    return jnp.concatenate(parts, axis=1) * g_full


def _layernorm(v, g, b):
    mu = jnp.mean(v, axis=-1, keepdims=True)
    vc = v - mu
    var = jnp.mean(vc * vc, axis=-1, keepdims=True)
    return vc * lax.rsqrt(var + EPS) * g + b


def _resident(shape):
    nd = len(shape)
    return pl.BlockSpec(shape, lambda *_: (0,) * nd, pipeline_mode=pl.Buffered(1))


def _layer(arr, layer, cols=None, col_block=0):
    shape = arr.shape[1:] if cols is None else arr.shape[1:-1] + (cols,)
    idx = (min(layer, arr.shape[0] - 1),) + (0,) * (len(shape) - 1) + (col_block,)
    return pl.BlockSpec((None,) + shape, lambda *_: idx, pipeline_mode=pl.Buffered(1))


def _params(semantics=None):
    return pltpu.CompilerParams(dimension_semantics=semantics,
                                vmem_limit_bytes=VMEM_LIMIT_BYTES)


O_BQ = 3 * D_MODEL
O_GD = O_BQ + 2 * B_KEY_DIM + D_MODEL
O_BZ = O_GD + GATE_RANK
O_CQ = O_BZ + D_MODEL
O_GL = O_CQ + 2 * D_MODEL
IN_COLS = O_GL + 3 * D_MODEL
WB_COLS = 2 * B_KEY_DIM + 2 * D_MODEL + GATE_RANK_PAD
TR_PREP = 128


def _split_w_in_kernel(wt_ref, wa_ref, wb_ref, wc_ref, wg_ref):
    def seg(lo, hi):
        return wt_ref[lo:hi, :].T.astype(BF16)

    wa_ref[...] = seg(0, O_BQ)
    wb_ref[:, :O_GD - O_BQ] = seg(O_BQ, O_GD)
    wb_ref[:, O_GD - O_BQ:O_GD - O_BQ + D_MODEL] = seg(O_BZ, O_CQ)
    gd = wt_ref[O_GD:O_GD + GATE_RANK_PAD, :].T
    lane = lax.broadcasted_iota(jnp.int32, gd.shape, 1)
    wb_ref[:, O_GD - O_BQ + D_MODEL:] = jnp.where(lane < GATE_RANK, gd, 0.0).astype(BF16)
    wc_ref[...] = seg(O_CQ, O_GL)
    wg_ref[...] = seg(O_GL, IN_COLS)


SPLIT_WIDTHS = (O_BQ, WB_COLS, O_GL - O_CQ, IN_COLS - O_GL)


def _split_w_in(w_in_t, layer):
    rows = w_in_t.shape[2]
    return pl.pallas_call(
        _split_w_in_kernel,
        grid=(rows // TR_PREP,),
        in_specs=[pl.BlockSpec((None, IN_COLS, TR_PREP), lambda i: (layer, 0, i))],
        out_specs=[pl.BlockSpec((None, TR_PREP, n), lambda i: (0, i, 0)) for n in SPLIT_WIDTHS],
        out_shape=[jax.ShapeDtypeStruct((1, rows, n), BF16) for n in SPLIT_WIDTHS],
        compiler_params=_params(("parallel",)),
        name="split_w_in",
    )(w_in_t)


def _cast_kernel(w_ref, o_ref):
    o_ref[...] = w_ref[...].astype(BF16)


def _cast_bf16(w):
    n, rows, cols = w.shape
    spec = pl.BlockSpec((None, rows, cols), lambda i: (i, 0, 0))
    return pl.pallas_call(
        _cast_kernel,
        grid=(n,),
        in_specs=[spec],
        out_specs=spec,
        out_shape=jax.ShapeDtypeStruct(w.shape, BF16),
        compiler_params=_params(("parallel",)),
        name="cast_bf16",
    )(w)


MEM_BATCHES = 2


def _memkv_kernel(mem_ref, g_ref, w_ref, mk_ref, mv_ref, mk16_ref, mv16_ref, w16_scr):
    @pl.when(pl.program_id(1) == 0)
    def _():
        w16_scr[...] = w_ref[...].astype(BF16)

    mem = mem_ref[...].reshape(MEM_BATCHES * N_MEM, D_MODEL)
    hn = _rms(mem, g_ref[...]).astype(BF16)
    mk = _dot(hn, w16_scr[:, :D_MODEL])
    mv = _dot(hn, w16_scr[:, D_MODEL:])
    for b in range(MEM_BATCHES):
        rows = slice(b * N_MEM, (b + 1) * N_MEM)
        for hd in range(C_HEADS):
            hs = slice(hd * C_HD, (hd + 1) * C_HD)
            mk_ref[b, :, hd, :] = mk[rows, hs]
            mv_ref[b, :, hd, :] = mv[rows, hs]
    mk16_ref[...] = mk.astype(BF16)
    mv16_ref[...] = mv.astype(BF16)


def _memkv(mem, g_mem, w_kv):
    batch = mem.shape[0]
    rows = MEM_BATCHES * N_MEM
    out5 = jax.ShapeDtypeStruct((DEPTH, batch, N_MEM, C_HEADS, C_HD), F32)
    out16 = jax.ShapeDtypeStruct((DEPTH, batch * N_MEM, D_MODEL), BF16)
    spec5 = pl.BlockSpec((None, MEM_BATCHES, N_MEM, C_HEADS, C_HD),
                         lambda l, b: (l, b, 0, 0, 0))
    spec16 = pl.BlockSpec((None, rows, D_MODEL), lambda l, b: (l, b, 0))
    return pl.pallas_call(
        _memkv_kernel,
        grid=(DEPTH, batch // MEM_BATCHES),
        in_specs=[
            pl.BlockSpec((MEM_BATCHES, N_MEM, D_MODEL), lambda l, b: (b, 0, 0)),
            pl.BlockSpec((None, 1, D_MODEL), lambda l, b: (l, 0, 0)),
            pl.BlockSpec((None, D_MODEL, 2 * D_MODEL), lambda l, b: (l, 0, 0)),
        ],
        out_specs=[spec5, spec5, spec16, spec16],
        out_shape=[out5, out5, out16, out16],
        scratch_shapes=[pltpu.VMEM((D_MODEL, 2 * D_MODEL), BF16)],
        compiler_params=_params(("arbitrary", "arbitrary")),
        name="memkv",
    )(mem, g_mem, w_kv)


def _prompt_a_kernel(x_ref, gn_ref, wa_ref, lng_ref, lnb_ref, ws_ref, bs_ref,
                     wg_ref, bg_ref, wbr_ref, qkd_ref, sv_ref, s0_ref, *rest):
    h_ref, m_ref, sn_ref, so_ref, s_scr = rest[-5:]
    _sample_gla_tile(qkd_ref, sv_ref, s0_ref, sn_ref, so_ref)
    n_chunks = TM // CHUNK_A
    h = _rms(x_ref[...], gn_ref[...]).astype(BF16)
    h_ref[...] = h
    v = _gelu(_dot(h, wa_ref[:, D_MODEL:2 * D_MODEL]))
    vn = _layernorm(v, lng_ref[...], lnb_ref[...]).astype(BF16)
    row = lax.broadcasted_iota(jnp.int32, (CHUNK_A, CHUNK_A), 0)
    col = lax.broadcasted_iota(jnp.int32, (CHUNK_A, CHUNK_A), 1)
    causal = row >= col
    for hd in range(A_HEADS):
        cs = slice(hd * A_HEAD_DIM, (hd + 1) * A_HEAD_DIM)
        w = jnp.where(causal, ws_ref[hd], 0.0).astype(BF16)
        vh = jnp.concatenate(
            [vn[c * CHUNK_A:(c + 1) * CHUNK_A, cs] for c in range(n_chunks)], axis=1)
        mixed = _dot(w, vh)
        for c in range(n_chunks):
            s_scr[c * CHUNK_A:(c + 1) * CHUNK_A, cs] = mixed[:, c * CHUNK_A:(c + 1) * CHUNK_A]
    u = _gelu(_dot(h, wa_ref[:, :D_MODEL]))
    z = _dot(h, wa_ref[:, 2 * D_MODEL:])
    bias = jnp.concatenate([bs_ref[...]] * n_chunks, axis=0)
    out_a = (u * (s_scr[...] + bias) * _silu(z)).astype(BF16)
    gate = _sigmoid(_dot(h, wg_ref[...]) + bg_ref[...])
    m_ref[...] = gate * _dot(out_a, wbr_ref[...])


def _prompt_a(x, qkd, sv, s0_all, s_all, p, l):
    n = x.shape[0]
    steps = n // TM
    assert qkd.shape[0] == steps, "one sample batch tile per prompt grid step"
    tile = lambda i: (i, 0)
    first = s_all is None
    n_slabs = s0_all.shape[0] if first else 1
    assert not first or l == 0
    state_blk = (BT_GLA, B_HEADS, B_HK, B_HV)
    args = [x, p["gn"], p["wa"], p["lng"], p["lnb"], p["ws"], p["bs_full"],
            p["wg"], p["bg"], p["wbr"], qkd, sv, s0_all]
    in_specs = [
        pl.BlockSpec((TM, D_MODEL), tile),
        _layer(p["gn"], l), _layer(p["wa"], l), _layer(p["lng"], l),
        _layer(p["lnb"], l), _layer(p["ws"], l), _layer(p["bs_full"], l),
        _layer(p["wg"], l, D_MODEL, 0), _layer(p["bg"], l, D_MODEL, 0),
        _layer(p["wbr"], N_BRANCH * l + 0),
        pl.BlockSpec((None, GLA_ROWS, B_HK), lambda i: (i, 0, 0)),
        pl.BlockSpec((None, BT_GLA, D_MODEL), lambda i: (i, 0, 0)),
        pl.BlockSpec((None,) + state_blk, lambda i: (l, i, 0, 0, 0)),
    ]
    aliases = {}
    if not first:
        in_specs.append(pl.BlockSpec(memory_space=pl.ANY))
        args.append(s_all)
        aliases = {len(args) - 1: 2}
    return pl.pallas_call(
        _prompt_a_kernel,
        grid=(steps,),
        in_specs=in_specs,
        out_specs=[pl.BlockSpec((TM, D_MODEL), tile), pl.BlockSpec((TM, D_MODEL), tile),
                   pl.BlockSpec((n_slabs,) + state_blk, lambda i: (l, i, 0, 0, 0)),
                   pl.BlockSpec((None, BT_GLA, D_MODEL), lambda i: (i, 0, 0))],
        out_shape=[jax.ShapeDtypeStruct((n, D_MODEL), BF16),
                   jax.ShapeDtypeStruct((n, D_MODEL), F32),
                   jax.ShapeDtypeStruct(s0_all.shape, F32),
                   jax.ShapeDtypeStruct(sv.shape, F32)],
        scratch_shapes=[pltpu.VMEM((TM, D_MODEL), F32)],
        input_output_aliases=aliases,
        compiler_params=_params(("arbitrary",)),
        name="prompt_a",
    )(*args)


def _prompt_b_kernel(h_ref, ma_ref, wb_ref, wgup_ref, bgu_ref, gnf_ref,
                     wg_ref, bg_ref, wbr_ref, *rest, split_every):
    if split_every:
        wt_ref, m_ref, st_ref, *split_out, o_scr = rest
        step = pl.program_id(0) * pl.num_programs(1) + pl.program_id(1)

        @pl.when(step % split_every == 0)
        def _():
            _split_w_in_kernel(wt_ref, *split_out)
    else:
        m_ref, st_ref, o_scr = rest

    @pl.when(pl.program_id(1) == 0)
    def _():
        st_ref[...] = jnp.zeros_like(st_ref)

    h = h_ref[...]
    q = _dot(h, wb_ref[:, 0:B_KEY_DIM]) * (B_HK ** -0.5)
    k = _dot(h, wb_ref[:, B_KEY_DIM:2 * B_KEY_DIM])
    v = _dot(h, wb_ref[:, 2 * B_KEY_DIM:2 * B_KEY_DIM + D_MODEL])
    gd = _dot(h, wb_ref[:, 2 * B_KEY_DIM + 2 * D_MODEL:]).astype(BF16)
    logit = _dot(gd, wgup_ref[...]) + bgu_ref[...]
    z_off = 2 * B_KEY_DIM + D_MODEL
    z = _dot(h, wb_ref[:, z_off:z_off + D_MODEL])
    gate_logit = _dot(h, wg_ref[...]) + bg_ref[...]

    n_chunks = TM // CHUNK_B
    chunk_rows = [slice(c * CHUNK_B, (c + 1) * CHUNK_B) for c in range(n_chunks)]
    row = lax.broadcasted_iota(jnp.int32, (CHUNK_B, CHUNK_B), 0)
    col = lax.broadcasted_iota(jnp.int32, (CHUNK_B, CHUNK_B), 1)
    causal = row >= col
    tri = jnp.where(causal, 1.0, 0.0).astype(BF16)

    g = _log_sigmoid(logit) * INV_GATE_NORMALIZER
    g_hi = g.astype(BF16)
    g_lo = (g - g_hi.astype(F32)).astype(BF16)
    bcum = jnp.concatenate(
        [_dot(tri, g_hi[rs]) + _dot(tri, g_lo[rs]) for rs in chunk_rows], axis=0)
    bcum3 = bcum.reshape(n_chunks, CHUNK_B, B_KEY_DIM)
    b_last3 = bcum3[:, CHUNK_B - 1:CHUNK_B, :]
    q_dec = (q * jnp.exp(bcum)).astype(BF16)
    k_dec = (k * jnp.exp(-bcum)).astype(BF16)
    k_tail = (k.reshape(bcum3.shape) * jnp.exp(b_last3 - bcum3)).astype(BF16).reshape(k.shape)
    decay3 = jnp.exp(b_last3)
    vb = v.astype(BF16)

    scores, kv_t = {}, {}
    for c, rs in enumerate(chunk_rows):
        for hd in range(B_HEADS):
            ks = slice(hd * B_HK, (hd + 1) * B_HK)
            vs = slice(hd * B_HV, (hd + 1) * B_HV)
            scores[c, hd] = jnp.where(
                causal, _dot_nt(q_dec[rs, ks], k_dec[rs, ks]), 0.0).astype(BF16)
            kv_t[c, hd] = _dot_tn(vb[rs, vs], k_tail[rs, ks])
    silu_z = _silu(z)
    gate = _sigmoid(gate_logit)

    for hd in range(B_HEADS):
        ks = slice(hd * B_HK, (hd + 1) * B_HK)
        vs = slice(hd * B_HV, (hd + 1) * B_HV)
        state_t = st_ref[hd]
        for c, rs in enumerate(chunk_rows):
            o_scr[rs, vs] = (_dot(scores[c, hd], vb[rs, vs])
                             + _dot_nt(q_dec[rs, ks], state_t.astype(BF16)))
            state_t = state_t * decay3[c][:, ks] + kv_t[c, hd]
        st_ref[hd] = state_t

    out_b = (_head_rms(o_scr[...], gnf_ref[...]) * silu_z).astype(BF16)
    m_ref[...] = ma_ref[...] + gate * _dot(out_b, wbr_ref[...])


def _prompt_b(h, ma, p, l, batch, w_in_t=None):
    n = h.shape[0]
    steps = n // batch // TM
    tile = lambda b, t: (b * steps + t, 0)
    args = [h, ma, p["wb"], p["wgup"], p["bgu"], p["gnf"], p["wg"], p["bg"], p["wbr"]]
    in_specs = [
        pl.BlockSpec((TM, D_MODEL), tile), pl.BlockSpec((TM, D_MODEL), tile),
        _layer(p["wb"], l), _layer(p["wgup"], l), _layer(p["bgu"], l),
        _layer(p["gnf"], l),
        _layer(p["wg"], l, D_MODEL, 1), _layer(p["bg"], l, D_MODEL, 1),
        _layer(p["wbr"], N_BRANCH * l + 1),
    ]
    out_specs = [
        pl.BlockSpec((TM, D_MODEL), tile),
        pl.BlockSpec((None, B_HEADS, B_HV, B_HK), lambda b, t: (b, 0, 0, 0)),
    ]
    out_shape = [jax.ShapeDtypeStruct((n, D_MODEL), F32),
                 jax.ShapeDtypeStruct((batch, B_HEADS, B_HV, B_HK), F32)]
    split_every = 0
    if w_in_t is not None:
        slabs = w_in_t.shape[2] // TR_PREP
        split_every = batch * steps // slabs
        assert split_every * slabs == batch * steps
        slab = lambda b, t: (b * steps + t) // split_every
        args.append(w_in_t)
        in_specs.append(pl.BlockSpec((None, IN_COLS, TR_PREP), lambda b, t: (l + 1, 0, slab(b, t))))
        out_specs += [pl.BlockSpec((None, TR_PREP, c), lambda b, t: (0, slab(b, t), 0))
                      for c in SPLIT_WIDTHS]
        out_shape += [jax.ShapeDtypeStruct((1, w_in_t.shape[2], c), BF16) for c in SPLIT_WIDTHS]
    res = pl.pallas_call(
        functools.partial(_prompt_b_kernel, split_every=split_every),
        grid=(batch, steps),
        in_specs=in_specs,
        out_specs=out_specs,
        out_shape=out_shape,
        scratch_shapes=[pltpu.VMEM((TM, D_MODEL), F32)],
        compiler_params=_params(("arbitrary", "arbitrary")),
        name="prompt_b",
    )(*args)
    return res[0], res[1], tuple(res[2:])


def _prompt_c_kernel(x_ref, h_ref, mab_ref, mk_ref, mv_ref, wc_ref,
                     wg_ref, bg_ref, wbr_ref, wout_ref, gf_ref, cq_ref, ck_ref, cv_ref,
                     xo_ref, oc_ref, *, final):
    h = h_ref[...]
    q = _dot(h, wc_ref[:, :D_MODEL]).astype(BF16)
    z = _dot(h, wc_ref[:, D_MODEL:])
    gate_logit = _dot(h, wg_ref[...]) + bg_ref[...]
    mk = mk_ref[...]
    mv = mv_ref[...]
    parts = []
    for hd in range(C_HEADS):
        hs = slice(hd * C_HD, (hd + 1) * C_HD)
        s = _dot_nt(q[:, hs], mk[:, hs]) * (C_HD ** -0.5)
        e = jnp.exp(s - jnp.max(s, axis=-1, keepdims=True))
        p = e / jnp.sum(e, axis=-1, keepdims=True)
        parts.append(_dot(p.astype(BF16), mv[:, hs]))
    out_c = (jnp.concatenate(parts, axis=1) * _silu(z)).astype(BF16)
    merged = mab_ref[...] + _sigmoid(gate_logit) * _dot(out_c, wbr_ref[...])
    x_new = x_ref[...] + _dot(merged.astype(BF16), wout_ref[...])
    xo_ref[...] = _rms(x_new, gf_ref[...]) if final else x_new
    _sample_att_tile(cq_ref, ck_ref, cv_ref, oc_ref)


def _prompt_c(x, h, mab, mk_all, mv_all, cq_packed, cache_k, cache_v, p, l, batch):
    n = x.shape[0]
    steps = n // batch // TM
    dec_batch = cq_packed.shape[0]
    assert dec_batch == batch * steps * BT_ATT, "one sample batch tile per prompt grid step"
    tile = lambda b, t: (b * steps + t, 0)
    mem = lambda b, t: (l, b, 0)
    att_row = lambda b, t: (b * steps + t, 0, 0)
    att_cache = lambda b, t: (l, b * steps + t, 0, 0, 0)
    args = (x, h, mab, mk_all, mv_all, p["wc"], p["wg"], p["bg"], p["wbr"], p["wout"], p["gf"],
            cq_packed, cache_k, cache_v)
    return pl.pallas_call(
        functools.partial(_prompt_c_kernel, final=l == DEPTH - 1),
        grid=(batch, steps),
        in_specs=[
            pl.BlockSpec((TM, D_MODEL), tile), pl.BlockSpec((TM, D_MODEL), tile),
            pl.BlockSpec((TM, D_MODEL), tile),
            pl.BlockSpec((None, N_MEM, D_MODEL), mem), pl.BlockSpec((None, N_MEM, D_MODEL), mem),
            _layer(p["wc"], l),
            _layer(p["wg"], l, D_MODEL, 2), _layer(p["bg"], l, D_MODEL, 2),
            _layer(p["wbr"], N_BRANCH * l + 2), _layer(p["wout"], l),
            _resident(p["gf"].shape),
            pl.BlockSpec((BT_ATT, PACK_ROWS, 128), att_row),
            pl.BlockSpec((None, BT_ATT, N_MEM, PACK_ROWS, 128), att_cache),
            pl.BlockSpec((None, BT_ATT, N_MEM, PACK_ROWS, 128), att_cache),
        ],
        out_specs=[pl.BlockSpec((TM, D_MODEL), tile),
                   pl.BlockSpec((BT_ATT, PACK_ROWS, 128), att_row)],
        out_shape=[jax.ShapeDtypeStruct((n, D_MODEL), F32),
                   jax.ShapeDtypeStruct((dec_batch, PACK_ROWS, 128), F32)],
        compiler_params=_params(("parallel", "arbitrary")),
        name="prompt_c",
    )(*args)


def _sample_front_kernel(x_ref, gn_ref, wa_ref, wb_ref, wc_ref, lng_ref, lnb_ref,
                         ws0_ref, bs0_ref, wgup_ref, bgu_ref,
                         h_ref, vn_ref, oa_ref, q_ref, k_ref, eg_ref, v_ref,
                         szb_ref, cq_ref, szc_ref):
    h = _rms(x_ref[...], gn_ref[...]).astype(BF16)
    h_ref[...] = h
    u = _gelu(_dot(h, wa_ref[:, :D_MODEL]))
    vn = _layernorm(_gelu(_dot(h, wa_ref[:, D_MODEL:2 * D_MODEL])), lng_ref[...], lnb_ref[...])
    vn_ref[...] = vn
    z = _dot(h, wa_ref[:, 2 * D_MODEL:])
    oa_ref[...] = (u * (vn * ws0_ref[...] + bs0_ref[...]) * _silu(z)).astype(BF16)
    q_ref[...] = _dot(h, wb_ref[:, 0:B_KEY_DIM]) * (B_HK ** -0.5)
    k_ref[...] = _dot(h, wb_ref[:, B_KEY_DIM:2 * B_KEY_DIM])
    v_ref[...] = _dot(h, wb_ref[:, 2 * B_KEY_DIM:2 * B_KEY_DIM + D_MODEL])
    szb_ref[...] = _silu(_dot(h, wb_ref[:, 2 * B_KEY_DIM + D_MODEL:2 * B_KEY_DIM + 2 * D_MODEL]))
    gd = _dot(h, wb_ref[:, 2 * B_KEY_DIM + 2 * D_MODEL:]).astype(BF16)
    g = _log_sigmoid(_dot(gd, wgup_ref[...]) + bgu_ref[...]) * INV_GATE_NORMALIZER
    eg_ref[...] = jnp.exp(g)
    cq_ref[...] = _dot(h, wc_ref[:, :D_MODEL]) * (C_HD ** -0.5)
    szc_ref[...] = _silu(_dot(h, wc_ref[:, D_MODEL:]))


def _sample_front(x, p, l):
    n = x.shape[0]
    wide = jax.ShapeDtypeStruct((n, D_MODEL), F32)
    half = jax.ShapeDtypeStruct((n, B_KEY_DIM), F32)
    wide16 = jax.ShapeDtypeStruct((n, D_MODEL), BF16)
    out_shape = [wide16, wide, wide16, half, half, half, wide, wide, wide, wide]
    names = ("gn", "wa", "wb", "wc", "lng", "lnb", "ws0", "bs0", "wgup", "bgu")
    return pl.pallas_call(
        _sample_front_kernel,
        grid=(1,),
        in_specs=[_resident(x.shape)] + [_layer(p[k], l) for k in names],
        out_specs=[pl.BlockSpec(s.shape, lambda i: (0, 0)) for s in out_shape],
        out_shape=out_shape,
        compiler_params=_params(("arbitrary",)),
        name="sample_front",
    )(x, *(p[k] for k in names))


GLA_ROWS = 3 * B_HEADS * BT_GLA


def _sample_gla_tile(qkd_ref, v_ref, s0_ref, s_ref, o_ref):
    pad = jnp.zeros((B_HK - GLA_ROWS, B_HK), F32)
    cols = jnp.concatenate([qkd_ref[...], pad], axis=0).T
    for j in range(BT_GLA):
        for hd in range(B_HEADS):
            vs = slice(hd * B_HV, (hd + 1) * B_HV)
            iq = hd * BT_GLA + j
            ik = (B_HEADS + hd) * BT_GLA + j
            ie = (2 * B_HEADS + hd) * BT_GLA + j
            state = (cols[:, ie:ie + 1] * s0_ref[j, hd]
                     + cols[:, ik:ik + 1] * v_ref[j:j + 1, vs])
            s_ref[0, j, hd] = state
            o_ref[j:j + 1, vs] = jnp.sum(cols[:, iq:iq + 1] * state, axis=0, keepdims=True)
    if s_ref.shape[0] > 1:
        s_ref[1:] = jnp.zeros((s_ref.shape[0] - 1,) + s_ref.shape[1:], F32)


def _stack_gla_rows(q, k, decay):
    n = q.shape[0]
    a = jnp.stack([q, k, decay]).reshape(3, n // BT_GLA, BT_GLA, B_HEADS, B_HK)
    return a.transpose(1, 0, 3, 2, 4).reshape(n // BT_GLA, GLA_ROWS, B_HK)


PACK_ROWS = C_HEADS * C_HD // 128


def _pack_heads(a):
    lead = a.shape[:-2]
    a = a.reshape(*lead, C_HEADS, C_HD // 128, 128)
    return jnp.swapaxes(a, -2, -3).reshape(*lead, PACK_ROWS, 128)


def _unpack_heads(a):
    lead = a.shape[:-2]
    a = a.reshape(*lead, C_HD // 128, C_HEADS, 128)
    return jnp.swapaxes(a, -2, -3).reshape(*lead, C_HEADS, C_HD)


def _sample_att_tile(cq_ref, mk_ref, mv_ref, o_ref):
    for j in range(BT_ATT):
        q = cq_ref[j]
        r = jnp.sum(mk_ref[j] * q[None], axis=-1, keepdims=True)
        s = r + pltpu.roll(r, C_HEADS, 1)
        e = jnp.exp(s - jnp.max(s, axis=0, keepdims=True))
        den = jnp.sum(e, axis=0)
        o_ref[j] = jnp.sum(e * mv_ref[j], axis=0) / den


def _sample_back_kernel(x_ref, h_ref, oa_ref, og_ref, szb_ref, oc_ref, szc_ref, gnf_ref,
                        wg_ref, bg_ref, wbr_ref, wout_ref, gf_ref, xo_ref, *, final):
    h = h_ref[...]
    out_b = (_head_rms(og_ref[...], gnf_ref[...]) * szb_ref[...]).astype(BF16)
    out_c = (oc_ref[...] * szc_ref[...]).astype(BF16)
    merged = None
    for i, br in enumerate((oa_ref[...], out_b, out_c)):
        cs = slice(i * D_MODEL, (i + 1) * D_MODEL)
        gate = _sigmoid(_dot(h, wg_ref[:, cs]) + bg_ref[:, cs])
        term = gate * _dot(br, wbr_ref[i])
        merged = term if merged is None else merged + term
    x_new = x_ref[...] + _dot(merged.astype(BF16), wout_ref[...])
    xo_ref[...] = _rms(x_new, gf_ref[...]) if final else x_new


def _sample_back(x, h, oa, og, szb, oc, szc, p, l):
    acts = (x, h, oa, og, szb, oc, szc)
    branch_w = pl.BlockSpec((N_BRANCH,) + p["wbr"].shape[1:], lambda i: (l, 0, 0),
                            pipeline_mode=pl.Buffered(1))
    return pl.pallas_call(
        functools.partial(_sample_back_kernel, final=l == DEPTH - 1),
        grid=(1,),
        in_specs=[_resident(a.shape) for a in acts] + [
            _layer(p["gnf"], l), _layer(p["wg"], l), _layer(p["bg"], l), branch_w,
            _layer(p["wout"], l), _resident(p["gf"].shape)],
        out_specs=pl.BlockSpec(x.shape, lambda i: (0, 0)),
        out_shape=jax.ShapeDtypeStruct(x.shape, F32),
        compiler_params=_params(("arbitrary",)),
        name="sample_back",
    )(*acts, p["gnf"], p["wg"], p["bg"], p["wbr"], p["wout"], p["gf"])


def kernel(x_prompt, x_sample, state_gla, cache_mem_k, cache_mem_v, mem_prompt, g_norm, w_in, b_gate, w_s, b_s, ln_v_g, ln_v_b, w_g_up, b_g, gn_g, g_mem, w_mem_kv, w_branch, w_out, g_final):
    batch, seq, _ = x_prompt.shape
    dec_batch = x_sample.shape[0]
    rows = lambda a: a.reshape(DEPTH, 1, -1)

    w_in_t = jnp.swapaxes(w_in, 1, 2)
    w_split = _split_w_in(w_in_t, 0)
    p = {
        "wbr": _cast_bf16(w_branch.reshape(DEPTH * N_BRANCH, D_MODEL, D_MODEL)),
        "wout": _cast_bf16(w_out),
        "wgup": jnp.pad(w_g_up, ((0, 0), (0, GATE_RANK_PAD - GATE_RANK), (0, 0))).astype(BF16),
        "gn": rows(g_norm), "lng": rows(ln_v_g), "lnb": rows(ln_v_b), "bgu": rows(b_g),
        "bg": rows(b_gate), "gnf": rows(jnp.tile(gn_g, (1, B_HEADS))),
        "ws": w_s,
        "bs_full": jnp.repeat(jnp.swapaxes(b_s, 1, 2), A_HEAD_DIM, axis=2),
        "ws0": rows(jnp.repeat(w_s[:, :, 0, 0], A_HEAD_DIM, axis=1)),
        "bs0": rows(jnp.repeat(b_s[:, :, 0], A_HEAD_DIM, axis=1)),
        "gf": g_final.reshape(1, D_MODEL),
    }

    mk_out, mv_out, mk_all, mv_all = _memkv(
        mem_prompt, g_mem.reshape(DEPTH, 1, D_MODEL), w_mem_kv)

    xp = x_prompt.reshape(batch * seq, D_MODEL)
    xs = x_sample.reshape(dec_batch, D_MODEL)
    cache_k = _pack_heads(cache_mem_k)
    cache_v = _pack_heads(cache_mem_v)
    gla_p, v_s = [], []
    gla_s = None
    for l in range(DEPTH):
        p.update(zip(("wa", "wb", "wc", "wg"), w_split))
        (hs_, vn, oa, q, k, eg, v, szb, cq, szc) = _sample_front(xs, p, l)
        cq_packed = _pack_heads(cq.reshape(dec_batch, C_HEADS, C_HD))

        qkd = _stack_gla_rows(q, k, eg)
        sv = v.reshape(dec_batch // BT_GLA, BT_GLA, D_MODEL)

        h, m, gla_s, og = _prompt_a(xp, qkd, sv, state_gla, gla_s, p, l)
        m, st, w_next = _prompt_b(h, m, p, l, batch, w_in_t if l + 1 < DEPTH else None)
        xp, oc = _prompt_c(xp, h, m, mk_all, mv_all, cq_packed, cache_k, cache_v, p, l, batch)
        gla_p.append(jnp.swapaxes(st, -1, -2))

        og = og.reshape(dec_batch, D_MODEL)
        oc = _unpack_heads(oc).reshape(dec_batch, D_MODEL)
        xs = _sample_back(xs, hs_, oa, og, szb, oc, szc, p, l)
        v_s.append(vn)
        w_split = w_next

    y_prompt = xp.reshape(batch, seq, D_MODEL)
    y_sample = xs.reshape(dec_batch, 1, D_MODEL)
    return (y_prompt, y_sample, jnp.stack(gla_p), mk_out, mv_out, gla_s,
            jnp.stack(v_s).reshape(DEPTH, dec_batch, 1, A_HEADS, A_HEAD_DIM))
```

```python
import functools

import jax
import jax.numpy as jnp
from jax import lax
from jax.experimental import pallas as pl
from jax.experimental.pallas import tpu as pltpu

F32 = jnp.float32
BF16 = jnp.bfloat16

D_MODEL = 1024
DEPTH = 4
CHUNK_A = 128
A_HEADS = 8
A_HEAD_DIM = 128
B_HEADS = 4
B_KEY_DIM = 512
B_HK = 128
B_HV = 256
GATE_RANK = 16
GATE_RANK_PAD = 128
INV_GATE_NORMALIZER = 1.0 / 16.0
CHUNK_B = 64
N_MEM = 256
C_HEADS = 4
C_HD = 256
N_BRANCH = 3
EPS = 1e-6
LOG2_E = 1.4426950408889634

VMEM_LIMIT_BYTES = 56 * 1024 * 1024
TM = 512
BT_GLA = 4
BT_ATT = 4


def _dot(a, b):
    return jnp.dot(a, b, preferred_element_type=F32)


def _dot_nt(a, b):
    return lax.dot_general(a, b, (((1,), (1,)), ((), ())), preferred_element_type=F32)


def _dot_tn(a, b):
    return lax.dot_general(a, b, (((0,), (0,)), ((), ())), preferred_element_type=F32)


def _sigmoid(x):
    return 1.0 / (1.0 + jnp.exp(-x))


def _silu(x):
    return x * _sigmoid(x)


def _gelu(x):
    c = 0.7978845608028654
    return x * (0.5 * (1.0 + jnp.tanh(c * (x + 0.044715 * (x * x * x)))))


def _log_sigmoid(x):
    return jnp.minimum(x, 0.0) - jnp.log1p(jnp.exp(-jnp.abs(x)))


def _rms(x, g):
    return x * lax.rsqrt(jnp.mean(x * x, axis=-1, keepdims=True) + EPS) * g


def _head_rms(o, g_full):
    parts = []
    for hd in range(B_HEADS):
        oh = o[:, hd * B_HV:(hd + 1) * B_HV]
        parts.append(oh * lax.rsqrt(jnp.mean(oh * oh, axis=-1, keepdims=True) + EPS))
    return jnp.concatenate(parts, axis=1) * g_full


def _layernorm(v, g, b):
    mu = jnp.mean(v, axis=-1, keepdims=True)
    vc = v - mu
    var = jnp.mean(vc * vc, axis=-1, keepdims=True)
    return vc * lax.rsqrt(var + EPS) * g + b


def _resident(shape):
    nd = len(shape)
    return pl.BlockSpec(shape, lambda *_: (0,) * nd, pipeline_mode=pl.Buffered(1))


def _layer(arr, layer, cols=None, col_block=0):
    shape = arr.shape[1:] if cols is None else arr.shape[1:-1] + (cols,)
    idx = (min(layer, arr.shape[0] - 1),) + (0,) * (len(shape) - 1) + (col_block,)
    return pl.BlockSpec((None,) + shape, lambda *_: idx, pipeline_mode=pl.Buffered(1))


def _params(semantics=None):
    return pltpu.CompilerParams(dimension_semantics=semantics,
                                vmem_limit_bytes=VMEM_LIMIT_BYTES)


O_BQ = 3 * D_MODEL
O_GD = O_BQ + 2 * B_KEY_DIM + D_MODEL
O_BZ = O_GD + GATE_RANK
O_CQ = O_BZ + D_MODEL
O_GL = O_CQ + 2 * D_MODEL
IN_COLS = O_GL + 3 * D_MODEL
WB_COLS = 2 * B_KEY_DIM + 2 * D_MODEL + GATE_RANK_PAD
TR_PREP = 128


def _split_w_in_kernel(wt_ref, wa_ref, wb_ref, wc_ref, wg_ref):
    def seg(lo, hi):
        return wt_ref[lo:hi, :].T.astype(BF16)

    wa_ref[...] = seg(0, O_BQ)
    wb_ref[:, :O_GD - O_BQ] = seg(O_BQ, O_GD)
    wb_ref[:, O_GD - O_BQ:O_GD - O_BQ + D_MODEL] = seg(O_BZ, O_CQ)
    gd = wt_ref[O_GD:O_GD + GATE_RANK_PAD, :].T
    lane = lax.broadcasted_iota(jnp.int32, gd.shape, 1)
    wb_ref[:, O_GD - O_BQ + D_MODEL:] = jnp.where(lane < GATE_RANK, gd, 0.0).astype(BF16)
    wc_ref[...] = seg(O_CQ, O_GL)
    wg_ref[...] = seg(O_GL, IN_COLS)


SPLIT_WIDTHS = (O_BQ, WB_COLS, O_GL - O_CQ, IN_COLS - O_GL)


def _split_w_in(w_in_t, layer):
    rows = w_in_t.shape[2]
    return pl.pallas_call(
        _split_w_in_kernel,
        grid=(rows // TR_PREP,),
        in_specs=[pl.BlockSpec((None, IN_COLS, TR_PREP), lambda i: (layer, 0, i))],
        out_specs=[pl.BlockSpec((None, TR_PREP, n), lambda i: (0, i, 0)) for n in SPLIT_WIDTHS],
        out_shape=[jax.ShapeDtypeStruct((1, rows, n), BF16) for n in SPLIT_WIDTHS],
        compiler_params=_params(("parallel",)),
        name="split_w_in",
    )(w_in_t)


def _cast_kernel(w_ref, o_ref):
    o_ref[...] = w_ref[...].astype(BF16)


def _cast_bf16(w, count):
    _, rows, cols = w.shape
    spec = pl.BlockSpec((None, rows, cols), lambda i: (i, 0, 0))
    return pl.pallas_call(
        _cast_kernel,
        grid=(count,),
        in_specs=[spec],
        out_specs=spec,
        out_shape=jax.ShapeDtypeStruct((count, rows, cols), BF16),
        compiler_params=_params(("parallel",)),
        name="cast_bf16",
    )(w)


MEM_BATCHES = 2


def _memkv_kernel(mem_ref, g_ref, w_ref, mk_ref, mv_ref, mk16_ref, mv16_ref, w16_scr):
    @pl.when(pl.program_id(1) == 0)
    def _():
        w16_scr[...] = w_ref[...].astype(BF16)

    mem = mem_ref[...].reshape(MEM_BATCHES * N_MEM, D_MODEL)
    hn = _rms(mem, g_ref[...]).astype(BF16)
    mk = _dot(hn, w16_scr[:, :D_MODEL])
    mv = _dot(hn, w16_scr[:, D_MODEL:])
    for b in range(MEM_BATCHES):
        rows = slice(b * N_MEM, (b + 1) * N_MEM)
        for hd in range(C_HEADS):
            hs = slice(hd * C_HD, (hd + 1) * C_HD)
            mk_ref[b, :, hd, :] = mk[rows, hs]
            mv_ref[b, :, hd, :] = mv[rows, hs]
    mk16_ref[...] = mk.astype(BF16)
    mv16_ref[...] = mv.astype(BF16)


def _memkv(mem, g_mem, w_kv):
    batch = mem.shape[0]
    rows = MEM_BATCHES * N_MEM
    out5 = jax.ShapeDtypeStruct((DEPTH, batch, N_MEM, C_HEADS, C_HD), F32)
    out16 = jax.ShapeDtypeStruct((DEPTH, batch * N_MEM, D_MODEL), BF16)
    spec5 = pl.BlockSpec((None, MEM_BATCHES, N_MEM, C_HEADS, C_HD),
                         lambda l, b: (l, b, 0, 0, 0))
    spec16 = pl.BlockSpec((None, rows, D_MODEL), lambda l, b: (l, b, 0))
    return pl.pallas_call(
        _memkv_kernel,
        grid=(DEPTH, batch // MEM_BATCHES),
        in_specs=[
            pl.BlockSpec((MEM_BATCHES, N_MEM, D_MODEL), lambda l, b: (b, 0, 0)),
            pl.BlockSpec((None, 1, D_MODEL), lambda l, b: (l, 0, 0)),
            pl.BlockSpec((None, D_MODEL, 2 * D_MODEL), lambda l, b: (l, 0, 0)),
        ],
        out_specs=[spec5, spec5, spec16, spec16],
        out_shape=[out5, out5, out16, out16],
        scratch_shapes=[pltpu.VMEM((D_MODEL, 2 * D_MODEL), BF16)],
        compiler_params=_params(("arbitrary", "arbitrary")),
        name="memkv",
    )(mem, g_mem, w_kv)


def _prompt_a_kernel(x_ref, gn_ref, wa_ref, lng_ref, lnb_ref, ws_ref, bs_ref,
                     wg_ref, bg_ref, wbr_ref, qkd_ref, sv_ref, s0_ref, *rest,
                     aliased, cast_rows):
    rest = list(rest)
    s_scr = rest.pop()
    if cast_rows:
        wout_next, wbr_next = rest.pop(), rest.pop()
    so_ref, sn_ref, m_ref, h_ref = rest.pop(), rest.pop(), rest.pop(), rest.pop()
    if aliased:
        rest.pop()
    if cast_rows:
        wout_src, wbr_src = rest.pop(), rest.pop()

        @pl.when(pl.program_id(0) < cast_rows)
        def _():
            wbr_next[...] = wbr_src[...].astype(BF16)

        @pl.when(pl.program_id(0) >= cast_rows)
        def _():
            wout_next[...] = wout_src[...].astype(BF16)

    _sample_gla_tile(qkd_ref, sv_ref, s0_ref, sn_ref, so_ref)
    n_chunks = TM // CHUNK_A
    h = _rms(x_ref[...], gn_ref[...]).astype(BF16)
    h_ref[...] = h
    v = _gelu(_dot(h, wa_ref[:, D_MODEL:2 * D_MODEL]))
    vn = _layernorm(v, lng_ref[...], lnb_ref[...]).astype(BF16)
    row = lax.broadcasted_iota(jnp.int32, (CHUNK_A, CHUNK_A), 0)
    col = lax.broadcasted_iota(jnp.int32, (CHUNK_A, CHUNK_A), 1)
    causal = row >= col
    for hd in range(A_HEADS):
        cs = slice(hd * A_HEAD_DIM, (hd + 1) * A_HEAD_DIM)
        w = jnp.where(causal, ws_ref[hd], 0.0).astype(BF16)
        vh = jnp.concatenate(
            [vn[c * CHUNK_A:(c + 1) * CHUNK_A, cs] for c in range(n_chunks)], axis=1)
        mixed = _dot(w, vh)
        for c in range(n_chunks):
            s_scr[c * CHUNK_A:(c + 1) * CHUNK_A, cs] = mixed[:, c * CHUNK_A:(c + 1) * CHUNK_A]
    u = _gelu(_dot(h, wa_ref[:, :D_MODEL]))
    z = _dot(h, wa_ref[:, 2 * D_MODEL:])
    bias = jnp.concatenate([bs_ref[...]] * n_chunks, axis=0)
    out_a = (u * (s_scr[...] + bias) * _silu(z)).astype(BF16)
    gate = _sigmoid(_dot(h, wg_ref[...]) + bg_ref[...])
    m_ref[...] = gate * _dot(out_a, wbr_ref[...])


def _prompt_a(x, qkd, sv, s0_all, s_all, p, l, w_branch=None, w_out=None):
    n = x.shape[0]
    steps = n // TM
    assert qkd.shape[0] == steps, "one sample batch tile per prompt grid step"
    tile = lambda i: (i, 0)
    first = s_all is None
    n_slabs = s0_all.shape[0] if first else 1
    assert not first or l == 0
    state_blk = (BT_GLA, B_HEADS, B_HK, B_HV)
    args = [x, p["gn"], p["wa"], p["lng"], p["lnb"], p["ws"], p["bs_full"],
            p["wg"], p["bg"], p["wbr"], qkd, sv, s0_all]
    in_specs = [
        pl.BlockSpec((TM, D_MODEL), tile),
        _layer(p["gn"], l), _layer(p["wa"], l), _layer(p["lng"], l),
        _layer(p["lnb"], l), _layer(p["ws"], l), _layer(p["bs_full"], l),
        _layer(p["wg"], l, D_MODEL, 0), _layer(p["bg"], l, D_MODEL, 0),
        _layer(p["wbr"], 0),
        pl.BlockSpec((None, GLA_ROWS, B_HK), lambda i: (i, 0, 0)),
        pl.BlockSpec((None, BT_GLA, D_MODEL), lambda i: (i, 0, 0)),
        pl.BlockSpec((None,) + state_blk, lambda i: (l, i, 0, 0, 0)),
    ]
    out_specs = [pl.BlockSpec((TM, D_MODEL), tile), pl.BlockSpec((TM, D_MODEL), tile),
                 pl.BlockSpec((n_slabs,) + state_blk, lambda i: (l, i, 0, 0, 0)),
                 pl.BlockSpec((None, BT_GLA, D_MODEL), lambda i: (i, 0, 0))]
    out_shape = [jax.ShapeDtypeStruct((n, D_MODEL), BF16),
                 jax.ShapeDtypeStruct((n, D_MODEL), F32),
                 jax.ShapeDtypeStruct(s0_all.shape, F32),
                 jax.ShapeDtypeStruct(sv.shape, F32)]
    cast_rows = 0
    if w_branch is not None:
        blk = (w_branch.shape[1] + w_out.shape[1]) // steps
        cast_rows = w_branch.shape[1] // blk
        assert blk * steps == w_branch.shape[1] + w_out.shape[1] and blk % 16 == 0
        br_blk = lambda i: jnp.minimum(i, cast_rows - 1)
        out_blk = lambda i: jnp.maximum(i - cast_rows, 0)
        args += [w_branch, w_out]
        in_specs += [pl.BlockSpec((None, blk, D_MODEL), lambda i: (l + 1, br_blk(i), 0)),
                     pl.BlockSpec((None, blk, D_MODEL), lambda i: (l + 1, out_blk(i), 0))]
        out_specs += [pl.BlockSpec((blk, D_MODEL), lambda i: (br_blk(i), 0)),
                      pl.BlockSpec((None, blk, D_MODEL), lambda i: (0, out_blk(i), 0))]
        out_shape += [jax.ShapeDtypeStruct(w_branch.shape[1:], BF16),
                      jax.ShapeDtypeStruct((1,) + w_out.shape[1:], BF16)]
    aliases = {}
    if not first:
        in_specs.append(pl.BlockSpec(memory_space=pl.ANY))
        args.append(s_all)
        aliases = {len(args) - 1: 2}
    return pl.pallas_call(
        functools.partial(_prompt_a_kernel, aliased=not first, cast_rows=cast_rows),
        grid=(steps,),
        in_specs=in_specs,
        out_specs=out_specs,
        out_shape=out_shape,
        scratch_shapes=[pltpu.VMEM((TM, D_MODEL), F32)],
        input_output_aliases=aliases,
        compiler_params=_params(("arbitrary",)),
        name="prompt_a",
    )(*args)


def _prompt_b_kernel(h_ref, ma_ref, wb_ref, wgup_ref, bgu_ref, gnf_ref,
                     wg_ref, bg_ref, wbr_ref, *rest, split_every):
    if split_every:
        wt_ref, m_ref, st_ref, *split_out, o_scr = rest
        step = pl.program_id(0) * pl.num_programs(1) + pl.program_id(1)

        @pl.when(step % split_every == 0)
        def _():
            _split_w_in_kernel(wt_ref, *split_out)
    else:
        m_ref, st_ref, o_scr = rest

    @pl.when(pl.program_id(1) == 0)
    def _():
        st_ref[...] = jnp.zeros_like(st_ref)

    h = h_ref[...]
    q = _dot(h, wb_ref[:, 0:B_KEY_DIM]) * (B_HK ** -0.5)
    k = _dot(h, wb_ref[:, B_KEY_DIM:2 * B_KEY_DIM])
    v = _dot(h, wb_ref[:, 2 * B_KEY_DIM:2 * B_KEY_DIM + D_MODEL])
    gd = _dot(h, wb_ref[:, 2 * B_KEY_DIM + 2 * D_MODEL:]).astype(BF16)
    logit = _dot(gd, wgup_ref[...]) + bgu_ref[...]
    z_off = 2 * B_KEY_DIM + D_MODEL
    z = _dot(h, wb_ref[:, z_off:z_off + D_MODEL])
    gate_logit = _dot(h, wg_ref[...]) + bg_ref[...]

    n_chunks = TM // CHUNK_B
    chunk_rows = [slice(c * CHUNK_B, (c + 1) * CHUNK_B) for c in range(n_chunks)]
    row = lax.broadcasted_iota(jnp.int32, (CHUNK_B, CHUNK_B), 0)
    col = lax.broadcasted_iota(jnp.int32, (CHUNK_B, CHUNK_B), 1)
    causal = row >= col
    tri = jnp.where(causal, 1.0, 0.0).astype(BF16)

    g = _log_sigmoid(logit) * INV_GATE_NORMALIZER
    g_hi = g.astype(BF16)
    g_lo = (g - g_hi.astype(F32)).astype(BF16)
    bcum = jnp.concatenate(
        [_dot(tri, g_hi[rs]) + _dot(tri, g_lo[rs]) for rs in chunk_rows], axis=0)
    bcum3 = bcum.reshape(n_chunks, CHUNK_B, B_KEY_DIM)
    b_last3 = bcum3[:, CHUNK_B - 1:CHUNK_B, :]
    q_dec = (q * jnp.exp(bcum)).astype(BF16)
    k_dec = (k * jnp.exp(-bcum)).astype(BF16)
    k_tail = (k.reshape(bcum3.shape) * jnp.exp(b_last3 - bcum3)).astype(BF16).reshape(k.shape)
    decay3 = jnp.exp(b_last3)
    vb = v.astype(BF16)

    scores, kv_t = {}, {}
    for c, rs in enumerate(chunk_rows):
        for hd in range(B_HEADS):
            ks = slice(hd * B_HK, (hd + 1) * B_HK)
            vs = slice(hd * B_HV, (hd + 1) * B_HV)
            scores[c, hd] = jnp.where(
                causal, _dot_nt(q_dec[rs, ks], k_dec[rs, ks]), 0.0).astype(BF16)
            kv_t[c, hd] = _dot_tn(vb[rs, vs], k_tail[rs, ks])
    silu_z = _silu(z)
    gate = _sigmoid(gate_logit)

    for hd in range(B_HEADS):
        ks = slice(hd * B_HK, (hd + 1) * B_HK)
        vs = slice(hd * B_HV, (hd + 1) * B_HV)
        state_t = st_ref[hd]
        for c, rs in enumerate(chunk_rows):
            o_scr[rs, vs] = (_dot(scores[c, hd], vb[rs, vs])
                             + _dot_nt(q_dec[rs, ks], state_t.astype(BF16)))
            state_t = state_t * decay3[c][:, ks] + kv_t[c, hd]
        st_ref[hd] = state_t

    out_b = (_head_rms(o_scr[...], gnf_ref[...]) * silu_z).astype(BF16)
    m_ref[...] = ma_ref[...] + gate * _dot(out_b, wbr_ref[...])


def _prompt_b(h, ma, p, l, batch, w_in_t=None):
    n = h.shape[0]
    steps = n // batch // TM
    tile = lambda b, t: (b * steps + t, 0)
    args = [h, ma, p["wb"], p["wgup"], p["bgu"], p["gnf"], p["wg"], p["bg"], p["wbr"]]
    in_specs = [
        pl.BlockSpec((TM, D_MODEL), tile), pl.BlockSpec((TM, D_MODEL), tile),
        _layer(p["wb"], l), _layer(p["wgup"], l), _layer(p["bgu"], l),
        _layer(p["gnf"], l),
        _layer(p["wg"], l, D_MODEL, 1), _layer(p["bg"], l, D_MODEL, 1),
        _layer(p["wbr"], 1),
    ]
    out_specs = [
        pl.BlockSpec((TM, D_MODEL), tile),
        pl.BlockSpec((None, B_HEADS, B_HV, B_HK), lambda b, t: (b, 0, 0, 0)),
    ]
    out_shape = [jax.ShapeDtypeStruct((n, D_MODEL), F32),
                 jax.ShapeDtypeStruct((batch, B_HEADS, B_HV, B_HK), F32)]
    split_every = 0
    if w_in_t is not None:
        slabs = w_in_t.shape[2] // TR_PREP
        split_every = batch * steps // slabs
        assert split_every * slabs == batch * steps
        slab = lambda b, t: (b * steps + t) // split_every
        args.append(w_in_t)
        in_specs.append(pl.BlockSpec((None, IN_COLS, TR_PREP), lambda b, t: (l + 1, 0, slab(b, t))))
        out_specs += [pl.BlockSpec((None, TR_PREP, c), lambda b, t: (0, slab(b, t), 0))
                      for c in SPLIT_WIDTHS]
        out_shape += [jax.ShapeDtypeStruct((1, w_in_t.shape[2], c), BF16) for c in SPLIT_WIDTHS]
    res = pl.pallas_call(
        functools.partial(_prompt_b_kernel, split_every=split_every),
        grid=(batch, steps),
        in_specs=in_specs,
        out_specs=out_specs,
        out_shape=out_shape,
        scratch_shapes=[pltpu.VMEM((TM, D_MODEL), F32)],
        compiler_params=_params(("arbitrary", "arbitrary")),
        name="prompt_b",
    )(*args)
    return res[0], res[1], tuple(res[2:])


def _prompt_c_kernel(x_ref, h_ref, mab_ref, mk_ref, mv_ref, wc_ref,
                     wg_ref, bg_ref, wbr_ref, wout_ref, gf_ref, cq_ref, ck_ref, cv_ref,
                     xo_ref, oc_ref, *, final):
    h = h_ref[...]
    q = _dot(h, wc_ref[:, :D_MODEL]).astype(BF16)
    z = _dot(h, wc_ref[:, D_MODEL:])
    gate_logit = _dot(h, wg_ref[...]) + bg_ref[...]
    mk = mk_ref[...]
    mv = mv_ref[...]
    parts = []
    for hd in range(C_HEADS):
        hs = slice(hd * C_HD, (hd + 1) * C_HD)
        s = _dot_nt(q[:, hs], mk[:, hs]) * (C_HD ** -0.5)
        e = jnp.exp(s - jnp.max(s, axis=-1, keepdims=True))
        p = e / jnp.sum(e, axis=-1, keepdims=True)
        parts.append(_dot(p.astype(BF16), mv[:, hs]))
    out_c = (jnp.concatenate(parts, axis=1) * _silu(z)).astype(BF16)
    merged = mab_ref[...] + _sigmoid(gate_logit) * _dot(out_c, wbr_ref[...])
    x_new = x_ref[...] + _dot(merged.astype(BF16), wout_ref[...])
    xo_ref[...] = _rms(x_new, gf_ref[...]) if final else x_new
    _sample_att_tile(cq_ref, ck_ref, cv_ref, oc_ref)


def _prompt_c(x, h, mab, mk_all, mv_all, cq_packed, cache_k, cache_v, p, l, batch):
    n = x.shape[0]
    steps = n // batch // TM
    dec_batch = cq_packed.shape[0]
    assert dec_batch == batch * steps * BT_ATT, "one sample batch tile per prompt grid step"
    tile = lambda b, t: (b * steps + t, 0)
    mem = lambda b, t: (l, b, 0)
    att_row = lambda b, t: (b * steps + t, 0, 0)
    att_cache = lambda b, t: (l, b * steps + t, 0, 0, 0)
    args = (x, h, mab, mk_all, mv_all, p["wc"], p["wg"], p["bg"], p["wbr"], p["wout"], p["gf"],
            cq_packed, cache_k, cache_v)
    return pl.pallas_call(
        functools.partial(_prompt_c_kernel, final=l == DEPTH - 1),
        grid=(batch, steps),
        in_specs=[
            pl.BlockSpec((TM, D_MODEL), tile), pl.BlockSpec((TM, D_MODEL), tile),
            pl.BlockSpec((TM, D_MODEL), tile),
            pl.BlockSpec((None, N_MEM, D_MODEL), mem), pl.BlockSpec((None, N_MEM, D_MODEL), mem),
            _layer(p["wc"], l),
            _layer(p["wg"], l, D_MODEL, 2), _layer(p["bg"], l, D_MODEL, 2),
            _layer(p["wbr"], 2), _layer(p["wout"], l),
            _resident(p["gf"].shape),
            pl.BlockSpec((BT_ATT, PACK_ROWS, 128), att_row),
            pl.BlockSpec((None, BT_ATT, N_MEM, PACK_ROWS, 128), att_cache),
            pl.BlockSpec((None, BT_ATT, N_MEM, PACK_ROWS, 128), att_cache),
        ],
        out_specs=[pl.BlockSpec((TM, D_MODEL), tile),
                   pl.BlockSpec((BT_ATT, PACK_ROWS, 128), att_row)],
        out_shape=[jax.ShapeDtypeStruct((n, D_MODEL), F32),
                   jax.ShapeDtypeStruct((dec_batch, PACK_ROWS, 128), F32)],
        compiler_params=_params(("parallel", "arbitrary")),
        name="prompt_c",
    )(*args)


def _sample_front_kernel(x_ref, gn_ref, wa_ref, wb_ref, wc_ref, lng_ref, lnb_ref,
                         ws0_ref, bs0_ref, wgup_ref, bgu_ref,
                         h_ref, vn_ref, oa_ref, q_ref, k_ref, eg_ref, v_ref,
                         szb_ref, cq_ref, szc_ref):
    h = _rms(x_ref[...], gn_ref[...]).astype(BF16)
    h_ref[...] = h
    u = _gelu(_dot(h, wa_ref[:, :D_MODEL]))
    vn = _layernorm(_gelu(_dot(h, wa_ref[:, D_MODEL:2 * D_MODEL])), lng_ref[...], lnb_ref[...])
    vn_ref[...] = vn
    z = _dot(h, wa_ref[:, 2 * D_MODEL:])
    oa_ref[...] = (u * (vn * ws0_ref[...] + bs0_ref[...]) * _silu(z)).astype(BF16)
    q_ref[...] = _dot(h, wb_ref[:, 0:B_KEY_DIM]) * (B_HK ** -0.5)
    k_ref[...] = _dot(h, wb_ref[:, B_KEY_DIM:2 * B_KEY_DIM])
    v_ref[...] = _dot(h, wb_ref[:, 2 * B_KEY_DIM:2 * B_KEY_DIM + D_MODEL])
    szb_ref[...] = _silu(_dot(h, wb_ref[:, 2 * B_KEY_DIM + D_MODEL:2 * B_KEY_DIM + 2 * D_MODEL]))
    gd = _dot(h, wb_ref[:, 2 * B_KEY_DIM + 2 * D_MODEL:]).astype(BF16)
    g = _log_sigmoid(_dot(gd, wgup_ref[...]) + bgu_ref[...]) * INV_GATE_NORMALIZER
    eg_ref[...] = jnp.exp(g)
    cq_ref[...] = _dot(h, wc_ref[:, :D_MODEL]) * (LOG2_E * C_HD ** -0.5)
    szc_ref[...] = _silu(_dot(h, wc_ref[:, D_MODEL:]))


def _sample_front(x, p, l):
    n = x.shape[0]
    wide = jax.ShapeDtypeStruct((n, D_MODEL), F32)
    half = jax.ShapeDtypeStruct((n, B_KEY_DIM), F32)
    wide16 = jax.ShapeDtypeStruct((n, D_MODEL), BF16)
    out_shape = [wide16, wide, wide16, half, half, half, wide, wide, wide, wide]
    names = ("gn", "wa", "wb", "wc", "lng", "lnb", "ws0", "bs0", "wgup", "bgu")
    return pl.pallas_call(
        _sample_front_kernel,
        grid=(1,),
        in_specs=[_resident(x.shape)] + [_layer(p[k], l) for k in names],
        out_specs=[pl.BlockSpec(s.shape, lambda i: (0, 0)) for s in out_shape],
        out_shape=out_shape,
        compiler_params=_params(("arbitrary",)),
        name="sample_front",
    )(x, *(p[k] for k in names))


GLA_ROWS = 3 * B_HEADS * BT_GLA


def _sample_gla_tile(qkd_ref, v_ref, s0_ref, s_ref, o_ref):
    pad = jnp.zeros((B_HK - GLA_ROWS, B_HK), F32)
    cols = jnp.concatenate([qkd_ref[...], pad], axis=0).T
    for j in range(BT_GLA):
        for hd in range(B_HEADS):
            vs = slice(hd * B_HV, (hd + 1) * B_HV)
            iq = hd * BT_GLA + j
            ik = (B_HEADS + hd) * BT_GLA + j
            ie = (2 * B_HEADS + hd) * BT_GLA + j
            state = (cols[:, ie:ie + 1] * s0_ref[j, hd]
                     + cols[:, ik:ik + 1] * v_ref[j:j + 1, vs])
            s_ref[0, j, hd] = state
            o_ref[j:j + 1, vs] = jnp.sum(cols[:, iq:iq + 1] * state, axis=0, keepdims=True)
    if s_ref.shape[0] > 1:
        s_ref[1:] = jnp.zeros((s_ref.shape[0] - 1,) + s_ref.shape[1:], F32)


def _stack_gla_rows(q, k, decay):
    n = q.shape[0]
    a = jnp.stack([q, k, decay]).reshape(3, n // BT_GLA, BT_GLA, B_HEADS, B_HK)
    return a.transpose(1, 0, 3, 2, 4).reshape(n // BT_GLA, GLA_ROWS, B_HK)


PACK_ROWS = C_HEADS * C_HD // 128


def _pack_heads(a):
    lead = a.shape[:-2]
    a = a.reshape(*lead, C_HEADS, C_HD // 128, 128)
    return jnp.swapaxes(a, -2, -3).reshape(*lead, PACK_ROWS, 128)


def _unpack_heads(a):
    lead = a.shape[:-2]
    a = a.reshape(*lead, C_HD // 128, C_HEADS, 128)
    return jnp.swapaxes(a, -2, -3).reshape(*lead, C_HEADS, C_HD)


def _sample_att_tile(cq_ref, mk_ref, mv_ref, o_ref):
    for j in range(BT_ATT):
        q = cq_ref[j]
        r = jnp.sum(mk_ref[j] * q[None], axis=-1, keepdims=True)
        r = jnp.broadcast_to(r, (N_MEM, PACK_ROWS, 128))
        s = r + pltpu.roll(r, C_HEADS, 1)
        e = jnp.exp2(s - jnp.max(s, axis=0, keepdims=True))
        o_ref[j] = jnp.sum(e * mv_ref[j], axis=0) / jnp.sum(e, axis=0)


def _sample_back_kernel(x_ref, h_ref, oa_ref, og_ref, szb_ref, oc_ref, szc_ref, gnf_ref,
                        wg_ref, bg_ref, wbr_ref, wout_ref, gf_ref, xo_ref, *, final):
    h = h_ref[...]
    out_b = (_head_rms(og_ref[...], gnf_ref[...]) * szb_ref[...]).astype(BF16)
    out_c = (oc_ref[...] * szc_ref[...]).astype(BF16)
    merged = None
    for i, br in enumerate((oa_ref[...], out_b, out_c)):
        cs = slice(i * D_MODEL, (i + 1) * D_MODEL)
        gate = _sigmoid(_dot(h, wg_ref[:, cs]) + bg_ref[:, cs])
        term = gate * _dot(br, wbr_ref[i])
        merged = term if merged is None else merged + term
    x_new = x_ref[...] + _dot(merged.astype(BF16), wout_ref[...])
    xo_ref[...] = _rms(x_new, gf_ref[...]) if final else x_new


def _sample_back(x, h, oa, og, szb, oc, szc, p, l):
    acts = (x, h, oa, og, szb, oc, szc)
    branch_w = _resident(p["wbr"].shape)
    return pl.pallas_call(
        functools.partial(_sample_back_kernel, final=l == DEPTH - 1),
        grid=(1,),
        in_specs=[_resident(a.shape) for a in acts] + [
            _layer(p["gnf"], l), _layer(p["wg"], l), _layer(p["bg"], l), branch_w,
            _layer(p["wout"], l), _resident(p["gf"].shape)],
        out_specs=pl.BlockSpec(x.shape, lambda i: (0, 0)),
        out_shape=jax.ShapeDtypeStruct(x.shape, F32),
        compiler_params=_params(("arbitrary",)),
        name="sample_back",
    )(*acts, p["gnf"], p["wg"], p["bg"], p["wbr"], p["wout"], p["gf"])


def kernel(x_prompt, x_sample, state_gla, cache_mem_k, cache_mem_v, mem_prompt, g_norm, w_in, b_gate, w_s, b_s, ln_v_g, ln_v_b, w_g_up, b_g, gn_g, g_mem, w_mem_kv, w_branch, w_out, g_final):
    batch, seq, _ = x_prompt.shape
    dec_batch = x_sample.shape[0]
    rows = lambda a: a.reshape(DEPTH, 1, -1)

    w_in_t = jnp.swapaxes(w_in, 1, 2)
    w_split = _split_w_in(w_in_t, 0)
    p = {
        "wgup": jnp.pad(w_g_up, ((0, 0), (0, GATE_RANK_PAD - GATE_RANK), (0, 0))).astype(BF16),
        "gn": rows(g_norm), "lng": rows(ln_v_g), "lnb": rows(ln_v_b), "bgu": rows(b_g),
        "bg": rows(b_gate), "gnf": rows(jnp.tile(gn_g, (1, B_HEADS))),
        "ws": w_s,
        "bs_full": jnp.repeat(jnp.swapaxes(b_s, 1, 2), A_HEAD_DIM, axis=2),
        "ws0": rows(jnp.repeat(w_s[:, :, 0, 0], A_HEAD_DIM, axis=1)),
        "bs0": rows(jnp.repeat(b_s[:, :, 0], A_HEAD_DIM, axis=1)),
        "gf": g_final.reshape(1, D_MODEL),
    }

    mk_out, mv_out, mk_all, mv_all = _memkv(
        mem_prompt, g_mem.reshape(DEPTH, 1, D_MODEL), w_mem_kv)

    xp = x_prompt.reshape(batch * seq, D_MODEL)
    xs = x_sample.reshape(dec_batch, D_MODEL)
    cache_k = _pack_heads(cache_mem_k)
    cache_v = _pack_heads(cache_mem_v)
    gla_p, v_s = [], []
    gla_s = None
    w_branch_rows = w_branch.reshape(DEPTH, N_BRANCH * D_MODEL, D_MODEL)
    w_proj = (_cast_bf16(w_branch.reshape(DEPTH * N_BRANCH, D_MODEL, D_MODEL), N_BRANCH),
              _cast_bf16(w_out, 1))
    for l in range(DEPTH):
        p.update(zip(("wa", "wb", "wc", "wg"), w_split))
        p.update(zip(("wbr", "wout"), w_proj))
        (hs_, vn, oa, q, k, eg, v, szb, cq, szc) = _sample_front(xs, p, l)
        cq_packed = _pack_heads(cq.reshape(dec_batch, C_HEADS, C_HD))

        qkd = _stack_gla_rows(q, k, eg)
        sv = v.reshape(dec_batch // BT_GLA, BT_GLA, D_MODEL)

        more = l + 1 < DEPTH
        h, m, gla_s, og, *proj_next = _prompt_a(
            xp, qkd, sv, state_gla, gla_s, p, l,
            w_branch_rows if more else None, w_out if more else None)
        m, st, w_next = _prompt_b(h, m, p, l, batch, w_in_t if more else None)
        xp, oc = _prompt_c(xp, h, m, mk_all, mv_all, cq_packed, cache_k, cache_v, p, l, batch)
        gla_p.append(jnp.swapaxes(st, -1, -2))

        og = og.reshape(dec_batch, D_MODEL)
        oc = _unpack_heads(oc).reshape(dec_batch, D_MODEL)
        xs = _sample_back(xs, hs_, oa, og, szb, oc, szc, p, l)
        v_s.append(vn)
        w_split = w_next
        if more:
            w_proj = (proj_next[0].reshape(N_BRANCH, D_MODEL, D_MODEL), proj_next[1])

    y_prompt = xp.reshape(batch, seq, D_MODEL)
    y_sample = xs.reshape(dec_batch, 1, D_MODEL)
    return (y_prompt, y_sample, jnp.stack(gla_p), mk_out, mv_out, gla_s,
            jnp.stack(v_s).reshape(DEPTH, dec_batch, 1, A_HEADS, A_HEAD_DIM))
```

```python
import functools

import jax
import jax.numpy as jnp
from jax import lax
from jax.experimental import pallas as pl
from jax.experimental.pallas import tpu as pltpu

F32 = jnp.float32
BF16 = jnp.bfloat16

D_MODEL = 1024
DEPTH = 4
CHUNK_A = 128
A_HEADS = 8
A_HEAD_DIM = 128
B_HEADS = 4
B_KEY_DIM = 512
B_HK = 128
B_HV = 256
GATE_RANK = 16
GATE_RANK_PAD = 128
INV_GATE_NORMALIZER = 1.0 / 16.0
CHUNK_B = 64
N_MEM = 256
C_HEADS = 4
C_HD = 256
N_BRANCH = 3
EPS = 1e-6
LOG2_E = 1.4426950408889634

VMEM_LIMIT_BYTES = 56 * 1024 * 1024
TM = 512
BT_GLA = 4
BT_ATT = 4


def _dot(a, b):
    return jnp.dot(a, b, preferred_element_type=F32)


def _dot_nt(a, b):
    return lax.dot_general(a, b, (((1,), (1,)), ((), ())), preferred_element_type=F32)


def _dot_tn(a, b):
    return lax.dot_general(a, b, (((0,), (0,)), ((), ())), preferred_element_type=F32)


def _sigmoid(x):
    return 1.0 / (1.0 + jnp.exp(-x))


def _silu(x):
    return x * _sigmoid(x)


def _gelu(x):
    c = 0.7978845608028654
    return x * (0.5 * (1.0 + jnp.tanh(c * (x + 0.044715 * (x * x * x)))))


def _log_sigmoid(x):
    return jnp.minimum(x, 0.0) - jnp.log1p(jnp.exp(-jnp.abs(x)))


def _rms(x, g):
    return x * lax.rsqrt(jnp.mean(x * x, axis=-1, keepdims=True) + EPS) * g


def _head_rms(o, g_full):
    parts = []
    for hd in range(B_HEADS):
        oh = o[:, hd * B_HV:(hd + 1) * B_HV]
        parts.append(oh * lax.rsqrt(jnp.mean(oh * oh, axis=-1, keepdims=True) + EPS))
    return jnp.concatenate(parts, axis=1) * g_full


def _layernorm(v, g, b):
    mu = jnp.mean(v, axis=-1, keepdims=True)
    vc = v - mu
    var = jnp.mean(vc * vc, axis=-1, keepdims=True)
    return vc * lax.rsqrt(var + EPS) * g + b


def _resident(shape):
    nd = len(shape)
    return pl.BlockSpec(shape, lambda *_: (0,) * nd, pipeline_mode=pl.Buffered(1))


def _layer(arr, layer, cols=None, col_block=0):
    shape = arr.shape[1:] if cols is None else arr.shape[1:-1] + (cols,)
    idx = (min(layer, arr.shape[0] - 1),) + (0,) * (len(shape) - 1) + (col_block,)
    return pl.BlockSpec((None,) + shape, lambda *_: idx, pipeline_mode=pl.Buffered(1))


def _params(semantics=None):
    return pltpu.CompilerParams(dimension_semantics=semantics,
                                vmem_limit_bytes=VMEM_LIMIT_BYTES)


O_BQ = 3 * D_MODEL
O_GD = O_BQ + 2 * B_KEY_DIM + D_MODEL
O_BZ = O_GD + GATE_RANK
O_CQ = O_BZ + D_MODEL
O_GL = O_CQ + 2 * D_MODEL
IN_COLS = O_GL + 3 * D_MODEL
WB_COLS = 2 * B_KEY_DIM + 2 * D_MODEL + GATE_RANK_PAD
TR_PREP = 128


def _split_w_in_kernel(wt_ref, wa_ref, wb_ref, wc_ref, wg_ref):
    def seg(lo, hi):
        return wt_ref[lo:hi, :].T.astype(BF16)

    wa_ref[...] = seg(0, O_BQ)
    wb_ref[:, :O_GD - O_BQ] = seg(O_BQ, O_GD)
    wb_ref[:, O_GD - O_BQ:O_GD - O_BQ + D_MODEL] = seg(O_BZ, O_CQ)
    gd = wt_ref[O_GD:O_GD + GATE_RANK_PAD, :].T
    lane = lax.broadcasted_iota(jnp.int32, gd.shape, 1)
    wb_ref[:, O_GD - O_BQ + D_MODEL:] = jnp.where(lane < GATE_RANK, gd, 0.0).astype(BF16)
    wc_ref[...] = seg(O_CQ, O_GL)
    wg_ref[...] = seg(O_GL, IN_COLS)


SPLIT_WIDTHS = (O_BQ, WB_COLS, O_GL - O_CQ, IN_COLS - O_GL)


def _split_w_in(w_in_t, layer):
    rows = w_in_t.shape[2]
    return pl.pallas_call(
        _split_w_in_kernel,
        grid=(rows // TR_PREP,),
        in_specs=[pl.BlockSpec((None, IN_COLS, TR_PREP), lambda i: (layer, 0, i))],
        out_specs=[pl.BlockSpec((None, TR_PREP, n), lambda i: (0, i, 0)) for n in SPLIT_WIDTHS],
        out_shape=[jax.ShapeDtypeStruct((1, rows, n), BF16) for n in SPLIT_WIDTHS],
        compiler_params=_params(("parallel",)),
        name="split_w_in",
    )(w_in_t)


def _cast_kernel(w_ref, o_ref):
    o_ref[...] = w_ref[...].astype(BF16)


def _cast_bf16(w, count):
    _, rows, cols = w.shape
    spec = pl.BlockSpec((None, rows, cols), lambda i: (i, 0, 0))
    return pl.pallas_call(
        _cast_kernel,
        grid=(count,),
        in_specs=[spec],
        out_specs=spec,
        out_shape=jax.ShapeDtypeStruct((count, rows, cols), BF16),
        compiler_params=_params(("parallel",)),
        name="cast_bf16",
    )(w)


MEM_BATCHES = 2


def _memkv_kernel(mem_ref, g_ref, w_ref, mk_ref, mv_ref, mk16_ref, mv16_ref, w16_scr):
    @pl.when(pl.program_id(1) == 0)
    def _():
        w16_scr[...] = w_ref[...].astype(BF16)

    mem = mem_ref[...].reshape(MEM_BATCHES * N_MEM, D_MODEL)
    hn = _rms(mem, g_ref[...]).astype(BF16)
    mk = _dot(hn, w16_scr[:, :D_MODEL])
    mv = _dot(hn, w16_scr[:, D_MODEL:])
    for b in range(MEM_BATCHES):
        rows = slice(b * N_MEM, (b + 1) * N_MEM)
        for hd in range(C_HEADS):
            hs = slice(hd * C_HD, (hd + 1) * C_HD)
            mk_ref[b, :, hd, :] = mk[rows, hs]
            mv_ref[b, :, hd, :] = mv[rows, hs]
    mk16_ref[...] = mk.astype(BF16)
    mv16_ref[...] = mv.astype(BF16)


def _memkv(mem, g_mem, w_kv):
    batch = mem.shape[0]
    rows = MEM_BATCHES * N_MEM
    out5 = jax.ShapeDtypeStruct((DEPTH, batch, N_MEM, C_HEADS, C_HD), F32)
    out16 = jax.ShapeDtypeStruct((DEPTH, batch * N_MEM, D_MODEL), BF16)
    spec5 = pl.BlockSpec((None, MEM_BATCHES, N_MEM, C_HEADS, C_HD),
                         lambda l, b: (l, b, 0, 0, 0))
    spec16 = pl.BlockSpec((None, rows, D_MODEL), lambda l, b: (l, b, 0))
    return pl.pallas_call(
        _memkv_kernel,
        grid=(DEPTH, batch // MEM_BATCHES),
        in_specs=[
            pl.BlockSpec((MEM_BATCHES, N_MEM, D_MODEL), lambda l, b: (b, 0, 0)),
            pl.BlockSpec((None, 1, D_MODEL), lambda l, b: (l, 0, 0)),
            pl.BlockSpec((None, D_MODEL, 2 * D_MODEL), lambda l, b: (l, 0, 0)),
        ],
        out_specs=[spec5, spec5, spec16, spec16],
        out_shape=[out5, out5, out16, out16],
        scratch_shapes=[pltpu.VMEM((D_MODEL, 2 * D_MODEL), BF16)],
        compiler_params=_params(("arbitrary", "arbitrary")),
        name="memkv",
    )(mem, g_mem, w_kv)


def _prompt_a_kernel(x_ref, gn_ref, wa_ref, lng_ref, lnb_ref, ws_ref, bs_ref,
                     wg_ref, bg_ref, wbr_ref, qkd_ref, sv_ref, s0_ref, *rest,
                     aliased, cast_rows):
    rest = list(rest)
    s_scr = rest.pop()
    if cast_rows:
        wout_next, wbr_next = rest.pop(), rest.pop()
    so_ref, sn_ref, m_ref, h_ref = rest.pop(), rest.pop(), rest.pop(), rest.pop()
    if aliased:
        rest.pop()
    if cast_rows:
        wout_src, wbr_src = rest.pop(), rest.pop()

        @pl.when(pl.program_id(0) < cast_rows)
        def _():
            wbr_next[...] = wbr_src[...].astype(BF16)

        @pl.when(pl.program_id(0) >= cast_rows)
        def _():
            wout_next[...] = wout_src[...].astype(BF16)

    _sample_gla_tile(qkd_ref, sv_ref, s0_ref, sn_ref, so_ref)
    n_chunks = TM // CHUNK_A
    h = _rms(x_ref[...], gn_ref[...]).astype(BF16)
    h_ref[...] = h
    v = _gelu(_dot(h, wa_ref[:, D_MODEL:2 * D_MODEL]))
    vn = _layernorm(v, lng_ref[...], lnb_ref[...]).astype(BF16)
    row = lax.broadcasted_iota(jnp.int32, (CHUNK_A, CHUNK_A), 0)
    col = lax.broadcasted_iota(jnp.int32, (CHUNK_A, CHUNK_A), 1)
    causal = row >= col
    for hd in range(A_HEADS):
        cs = slice(hd * A_HEAD_DIM, (hd + 1) * A_HEAD_DIM)
        w = jnp.where(causal, ws_ref[hd], 0.0).astype(BF16)
        vh = jnp.concatenate(
            [vn[c * CHUNK_A:(c + 1) * CHUNK_A, cs] for c in range(n_chunks)], axis=1)
        mixed = _dot(w, vh)
        for c in range(n_chunks):
            s_scr[c * CHUNK_A:(c + 1) * CHUNK_A, cs] = mixed[:, c * CHUNK_A:(c + 1) * CHUNK_A]
    u = _gelu(_dot(h, wa_ref[:, :D_MODEL]))
    z = _dot(h, wa_ref[:, 2 * D_MODEL:])
    bias = jnp.concatenate([bs_ref[...]] * n_chunks, axis=0)
    out_a = (u * (s_scr[...] + bias) * _silu(z)).astype(BF16)
    gate = _sigmoid(_dot(h, wg_ref[...]) + bg_ref[...])
    m_ref[...] = gate * _dot(out_a, wbr_ref[...])


def _prompt_a(x, qkd, sv, s0_all, s_all, p, l, w_branch=None, w_out=None):
    n = x.shape[0]
    steps = n // TM
    assert qkd.shape[0] == steps, "one sample batch tile per prompt grid step"
    tile = lambda i: (i, 0)
    first = s_all is None
    n_slabs = s0_all.shape[0] if first else 1
    assert not first or l == 0
    state_blk = (BT_GLA, B_HEADS, B_HK, B_HV)
    args = [x, p["gn"], p["wa"], p["lng"], p["lnb"], p["ws"], p["bs_full"],
            p["wg"], p["bg"], p["wbr"], qkd, sv, s0_all]
    in_specs = [
        pl.BlockSpec((TM, D_MODEL), tile),
        _layer(p["gn"], l), _layer(p["wa"], l), _layer(p["lng"], l),
        _layer(p["lnb"], l), _layer(p["ws"], l), _layer(p["bs_full"], l),
        _layer(p["wg"], l, D_MODEL, 0), _layer(p["bg"], l, D_MODEL, 0),
        _layer(p["wbr"], 0),
        pl.BlockSpec((None, GLA_ROWS, B_HK), lambda i: (i, 0, 0)),
        pl.BlockSpec((None, BT_GLA, D_MODEL), lambda i: (i, 0, 0)),
        pl.BlockSpec((None,) + state_blk, lambda i: (l, i, 0, 0, 0)),
    ]
    out_specs = [pl.BlockSpec((TM, D_MODEL), tile), pl.BlockSpec((TM, D_MODEL), tile),
                 pl.BlockSpec((n_slabs,) + state_blk, lambda i: (l, i, 0, 0, 0)),
                 pl.BlockSpec((None, BT_GLA, D_MODEL), lambda i: (i, 0, 0))]
    out_shape = [jax.ShapeDtypeStruct((n, D_MODEL), BF16),
                 jax.ShapeDtypeStruct((n, D_MODEL), F32),
                 jax.ShapeDtypeStruct(s0_all.shape, F32),
                 jax.ShapeDtypeStruct(sv.shape, F32)]
    cast_rows = 0
    if w_branch is not None:
        blk = (w_branch.shape[1] + w_out.shape[1]) // steps
        cast_rows = w_branch.shape[1] // blk
        assert blk * steps == w_branch.shape[1] + w_out.shape[1] and blk % 16 == 0
        br_blk = lambda i: jnp.minimum(i, cast_rows - 1)
        out_blk = lambda i: jnp.maximum(i - cast_rows, 0)
        args += [w_branch, w_out]
        in_specs += [pl.BlockSpec((None, blk, D_MODEL), lambda i: (l + 1, br_blk(i), 0)),
                     pl.BlockSpec((None, blk, D_MODEL), lambda i: (l + 1, out_blk(i), 0))]
        out_specs += [pl.BlockSpec((blk, D_MODEL), lambda i: (br_blk(i), 0)),
                      pl.BlockSpec((None, blk, D_MODEL), lambda i: (0, out_blk(i), 0))]
        out_shape += [jax.ShapeDtypeStruct(w_branch.shape[1:], BF16),
                      jax.ShapeDtypeStruct((1,) + w_out.shape[1:], BF16)]
    aliases = {}
    if not first:
        in_specs.append(pl.BlockSpec(memory_space=pl.ANY))
        args.append(s_all)
        aliases = {len(args) - 1: 2}
    return pl.pallas_call(
        functools.partial(_prompt_a_kernel, aliased=not first, cast_rows=cast_rows),
        grid=(steps,),
        in_specs=in_specs,
        out_specs=out_specs,
        out_shape=out_shape,
        scratch_shapes=[pltpu.VMEM((TM, D_MODEL), F32)],
        input_output_aliases=aliases,
        compiler_params=_params(("arbitrary",)),
        name="prompt_a",
    )(*args)


def _prompt_b_kernel(h_ref, ma_ref, wb_ref, wgup_ref, bgu_ref, gnf_ref,
                     wg_ref, bg_ref, wbr_ref, *rest, split_every):
    if split_every:
        wt_ref, m_ref, st_ref, *split_out, o_scr = rest
        step = pl.program_id(0) * pl.num_programs(1) + pl.program_id(1)

        @pl.when(step % split_every == 0)
        def _():
            _split_w_in_kernel(wt_ref, *split_out)
    else:
        m_ref, st_ref, o_scr = rest

    @pl.when(pl.program_id(1) == 0)
    def _():
        st_ref[...] = jnp.zeros_like(st_ref)

    h = h_ref[...]
    q = _dot(h, wb_ref[:, 0:B_KEY_DIM]) * (B_HK ** -0.5)
    k = _dot(h, wb_ref[:, B_KEY_DIM:2 * B_KEY_DIM])
    v = _dot(h, wb_ref[:, 2 * B_KEY_DIM:2 * B_KEY_DIM + D_MODEL])
    gd = _dot(h, wb_ref[:, 2 * B_KEY_DIM + 2 * D_MODEL:]).astype(BF16)
    logit = _dot(gd, wgup_ref[...]) + bgu_ref[...]
    z_off = 2 * B_KEY_DIM + D_MODEL
    z = _dot(h, wb_ref[:, z_off:z_off + D_MODEL])
    gate_logit = _dot(h, wg_ref[...]) + bg_ref[...]

    n_chunks = TM // CHUNK_B
    chunk_rows = [slice(c * CHUNK_B, (c + 1) * CHUNK_B) for c in range(n_chunks)]
    row = lax.broadcasted_iota(jnp.int32, (CHUNK_B, CHUNK_B), 0)
    col = lax.broadcasted_iota(jnp.int32, (CHUNK_B, CHUNK_B), 1)
    causal = row >= col
    tri = jnp.where(causal, 1.0, 0.0).astype(BF16)

    g = _log_sigmoid(logit) * INV_GATE_NORMALIZER
    g_hi = g.astype(BF16)
    g_lo = (g - g_hi.astype(F32)).astype(BF16)
    bcum = jnp.concatenate(
        [_dot(tri, g_hi[rs]) + _dot(tri, g_lo[rs]) for rs in chunk_rows], axis=0)
    bcum3 = bcum.reshape(n_chunks, CHUNK_B, B_KEY_DIM)
    b_last3 = bcum3[:, CHUNK_B - 1:CHUNK_B, :]
    q_dec = (q * jnp.exp(bcum)).astype(BF16)
    k_dec = (k * jnp.exp(-bcum)).astype(BF16)
    k_tail = (k.reshape(bcum3.shape) * jnp.exp(b_last3 - bcum3)).astype(BF16).reshape(k.shape)
    decay3 = jnp.exp(b_last3)
    vb = v.astype(BF16)

    scores, kv_t = {}, {}
    for c, rs in enumerate(chunk_rows):
        for hd in range(B_HEADS):
            ks = slice(hd * B_HK, (hd + 1) * B_HK)
            vs = slice(hd * B_HV, (hd + 1) * B_HV)
            scores[c, hd] = jnp.where(
                causal, _dot_nt(q_dec[rs, ks], k_dec[rs, ks]), 0.0).astype(BF16)
            kv_t[c, hd] = _dot_tn(vb[rs, vs], k_tail[rs, ks])
    silu_z = _silu(z)
    gate = _sigmoid(gate_logit)

    for hd in range(B_HEADS):
        ks = slice(hd * B_HK, (hd + 1) * B_HK)
        vs = slice(hd * B_HV, (hd + 1) * B_HV)
        state_t = st_ref[hd]
        for c, rs in enumerate(chunk_rows):
            o_scr[rs, vs] = (_dot(scores[c, hd], vb[rs, vs])
                             + _dot_nt(q_dec[rs, ks], state_t.astype(BF16)))
            state_t = state_t * decay3[c][:, ks] + kv_t[c, hd]
        st_ref[hd] = state_t

    out_b = (_head_rms(o_scr[...], gnf_ref[...]) * silu_z).astype(BF16)
    m_ref[...] = ma_ref[...] + gate * _dot(out_b, wbr_ref[...])


def _prompt_b(h, ma, p, l, batch, w_in_t=None):
    n = h.shape[0]
    steps = n // batch // TM
    tile = lambda b, t: (b * steps + t, 0)
    args = [h, ma, p["wb"], p["wgup"], p["bgu"], p["gnf"], p["wg"], p["bg"], p["wbr"]]
    in_specs = [
        pl.BlockSpec((TM, D_MODEL), tile), pl.BlockSpec((TM, D_MODEL), tile),
        _layer(p["wb"], l), _layer(p["wgup"], l), _layer(p["bgu"], l),
        _layer(p["gnf"], l),
        _layer(p["wg"], l, D_MODEL, 1), _layer(p["bg"], l, D_MODEL, 1),
        _layer(p["wbr"], 1),
    ]
    out_specs = [
        pl.BlockSpec((TM, D_MODEL), tile),
        pl.BlockSpec((None, B_HEADS, B_HV, B_HK), lambda b, t: (b, 0, 0, 0)),
    ]
    out_shape = [jax.ShapeDtypeStruct((n, D_MODEL), F32),
                 jax.ShapeDtypeStruct((batch, B_HEADS, B_HV, B_HK), F32)]
    split_every = 0
    if w_in_t is not None:
        slabs = w_in_t.shape[2] // TR_PREP
        split_every = batch * steps // slabs
        assert split_every * slabs == batch * steps
        slab = lambda b, t: (b * steps + t) // split_every
        args.append(w_in_t)
        in_specs.append(pl.BlockSpec((None, IN_COLS, TR_PREP), lambda b, t: (l + 1, 0, slab(b, t))))
        out_specs += [pl.BlockSpec((None, TR_PREP, c), lambda b, t: (0, slab(b, t), 0))
                      for c in SPLIT_WIDTHS]
        out_shape += [jax.ShapeDtypeStruct((1, w_in_t.shape[2], c), BF16) for c in SPLIT_WIDTHS]
    res = pl.pallas_call(
        functools.partial(_prompt_b_kernel, split_every=split_every),
        grid=(batch, steps),
        in_specs=in_specs,
        out_specs=out_specs,
        out_shape=out_shape,
        scratch_shapes=[pltpu.VMEM((TM, D_MODEL), F32)],
        compiler_params=_params(("arbitrary", "arbitrary")),
        name="prompt_b",
    )(*args)
    return res[0], res[1], tuple(res[2:])


def _prompt_c_kernel(x_ref, h_ref, mab_ref, mk_ref, mv_ref, wc_ref,
                     wg_ref, bg_ref, wbr_ref, wout_ref, gf_ref, cq_ref, ck_ref, cv_ref,
                     xo_ref, oc_ref, *, final):
    h = h_ref[...]
    q = _dot(h, wc_ref[:, :D_MODEL]).astype(BF16)
    z = _dot(h, wc_ref[:, D_MODEL:])
    gate_logit = _dot(h, wg_ref[...]) + bg_ref[...]
    mk = mk_ref[...]
    mv = mv_ref[...]
    parts = []
    for hd in range(C_HEADS):
        hs = slice(hd * C_HD, (hd + 1) * C_HD)
        s = _dot_nt(q[:, hs], mk[:, hs]) * (C_HD ** -0.5)
        e = jnp.exp(s - jnp.max(s, axis=-1, keepdims=True))
        p = e / jnp.sum(e, axis=-1, keepdims=True)
        parts.append(_dot(p.astype(BF16), mv[:, hs]))
    out_c = (jnp.concatenate(parts, axis=1) * _silu(z)).astype(BF16)
    merged = mab_ref[...] + _sigmoid(gate_logit) * _dot(out_c, wbr_ref[...])
    x_new = x_ref[...] + _dot(merged.astype(BF16), wout_ref[...])
    xo_ref[...] = _rms(x_new, gf_ref[...]) if final else x_new
    _sample_att_tile(cq_ref, ck_ref, cv_ref, oc_ref)


def _prompt_c(x, h, mab, mk_all, mv_all, cq_packed, cache_k, cache_v, p, l, batch):
    n = x.shape[0]
    steps = n // batch // TM
    dec_batch = cq_packed.shape[0]
    assert dec_batch == batch * steps * BT_ATT, "one sample batch tile per prompt grid step"
    tile = lambda b, t: (b * steps + t, 0)
    mem = lambda b, t: (l, b, 0)
    att_row = lambda b, t: (b * steps + t, 0, 0)
    att_cache = lambda b, t: (l, b * steps + t, 0, 0, 0)
    args = (x, h, mab, mk_all, mv_all, p["wc"], p["wg"], p["bg"], p["wbr"], p["wout"], p["gf"],
            cq_packed, cache_k, cache_v)
    return pl.pallas_call(
        functools.partial(_prompt_c_kernel, final=l == DEPTH - 1),
        grid=(batch, steps),
        in_specs=[
            pl.BlockSpec((TM, D_MODEL), tile), pl.BlockSpec((TM, D_MODEL), tile),
            pl.BlockSpec((TM, D_MODEL), tile),
            pl.BlockSpec((None, N_MEM, D_MODEL), mem), pl.BlockSpec((None, N_MEM, D_MODEL), mem),
            _layer(p["wc"], l),
            _layer(p["wg"], l, D_MODEL, 2), _layer(p["bg"], l, D_MODEL, 2),
            _layer(p["wbr"], 2), _layer(p["wout"], l),
            _resident(p["gf"].shape),
            pl.BlockSpec((BT_ATT, PACK_ROWS, 128), att_row),
            pl.BlockSpec((None, BT_ATT, N_MEM, PACK_ROWS, 128), att_cache),
            pl.BlockSpec((None, BT_ATT, N_MEM, PACK_ROWS, 128), att_cache),
        ],
        out_specs=[pl.BlockSpec((TM, D_MODEL), tile),
                   pl.BlockSpec((BT_ATT, PACK_ROWS, 128), att_row)],
        out_shape=[jax.ShapeDtypeStruct((n, D_MODEL), F32),
                   jax.ShapeDtypeStruct((dec_batch, PACK_ROWS, 128), F32)],
        compiler_params=_params(("parallel", "arbitrary")),
        name="prompt_c",
    )(*args)


def _sample_front_kernel(x_ref, gn_ref, wa_ref, wb_ref, wc_ref, lng_ref, lnb_ref,
                         ws0_ref, bs0_ref, wgup_ref, bgu_ref,
                         h_ref, vn_ref, oa_ref, qkd_ref, v_ref, szb_ref, cqp_ref, szc_ref):
    h = _rms(x_ref[...], gn_ref[...]).astype(BF16)
    h_ref[...] = h
    u = _gelu(_dot(h, wa_ref[:, :D_MODEL]))
    vn = _layernorm(_gelu(_dot(h, wa_ref[:, D_MODEL:2 * D_MODEL])), lng_ref[...], lnb_ref[...])
    vn_ref[...] = vn
    z = _dot(h, wa_ref[:, 2 * D_MODEL:])
    oa_ref[...] = (u * (vn * ws0_ref[...] + bs0_ref[...]) * _silu(z)).astype(BF16)
    q = _dot(h, wb_ref[:, 0:B_KEY_DIM]) * (B_HK ** -0.5)
    k = _dot(h, wb_ref[:, B_KEY_DIM:2 * B_KEY_DIM])
    v_ref[...] = _dot(h, wb_ref[:, 2 * B_KEY_DIM:2 * B_KEY_DIM + D_MODEL])
    szb_ref[...] = _silu(_dot(h, wb_ref[:, 2 * B_KEY_DIM + D_MODEL:2 * B_KEY_DIM + 2 * D_MODEL]))
    gd = _dot(h, wb_ref[:, 2 * B_KEY_DIM + 2 * D_MODEL:]).astype(BF16)
    g = _log_sigmoid(_dot(gd, wgup_ref[...]) + bgu_ref[...]) * INV_GATE_NORMALIZER
    for a, arr in enumerate((q, k, jnp.exp(g))):
        for hd in range(B_HEADS):
            qkd_ref[:, a * B_HEADS + hd, :] = arr[:, hd * B_HK:(hd + 1) * B_HK]
    cq = _dot(h, wc_ref[:, :D_MODEL]) * (LOG2_E * C_HD ** -0.5)
    for r in range(PACK_ROWS):
        lo = (r % C_HEADS) * C_HD + (r // C_HEADS) * 128
        cqp_ref[:, r, :] = cq[:, lo:lo + 128]
    szc_ref[...] = _silu(_dot(h, wc_ref[:, D_MODEL:]))


def _sample_front(x, p, l):
    n = x.shape[0]
    wide = jax.ShapeDtypeStruct((n, D_MODEL), F32)
    wide16 = jax.ShapeDtypeStruct((n, D_MODEL), BF16)
    qkd = jax.ShapeDtypeStruct((n, 3 * B_HEADS, B_HK), F32)
    cqp = jax.ShapeDtypeStruct((n, PACK_ROWS, 128), F32)
    out_shape = [wide16, wide, wide16, qkd, wide, wide, cqp, wide]
    names = ("gn", "wa", "wb", "wc", "lng", "lnb", "ws0", "bs0", "wgup", "bgu")
    return pl.pallas_call(
        _sample_front_kernel,
        grid=(1,),
        in_specs=[_resident(x.shape)] + [_layer(p[k], l) for k in names],
        out_specs=[pl.BlockSpec(s.shape, lambda i, nd=len(s.shape): (0,) * nd)
                   for s in out_shape],
        out_shape=out_shape,
        compiler_params=_params(("arbitrary",)),
        name="sample_front",
    )(x, *(p[k] for k in names))


GLA_ROWS = 3 * B_HEADS * BT_GLA


def _sample_gla_tile(qkd_ref, v_ref, s0_ref, s_ref, o_ref):
    pad = jnp.zeros((B_HK - GLA_ROWS, B_HK), F32)
    cols = jnp.concatenate([qkd_ref[...], pad], axis=0).T
    for j in range(BT_GLA):
        for hd in range(B_HEADS):
            vs = slice(hd * B_HV, (hd + 1) * B_HV)
            iq = hd * BT_GLA + j
            ik = (B_HEADS + hd) * BT_GLA + j
            ie = (2 * B_HEADS + hd) * BT_GLA + j
            state = (cols[:, ie:ie + 1] * s0_ref[j, hd]
                     + cols[:, ik:ik + 1] * v_ref[j:j + 1, vs])
            s_ref[0, j, hd] = state
            o_ref[j:j + 1, vs] = jnp.sum(cols[:, iq:iq + 1] * state, axis=0, keepdims=True)
    if s_ref.shape[0] > 1:
        s_ref[1:] = jnp.zeros((s_ref.shape[0] - 1,) + s_ref.shape[1:], F32)


def _stack_gla_rows(qkd):
    n, groups, _ = qkd.shape
    a = qkd.reshape(n // BT_GLA, BT_GLA, groups, B_HK)
    return jnp.swapaxes(a, 1, 2).reshape(n // BT_GLA, GLA_ROWS, B_HK)


PACK_ROWS = C_HEADS * C_HD // 128


def _pack_heads(a):
    lead = a.shape[:-2]
    a = a.reshape(*lead, C_HEADS, C_HD // 128, 128)
    return jnp.swapaxes(a, -2, -3).reshape(*lead, PACK_ROWS, 128)


def _sample_att_tile(cq_ref, mk_ref, mv_ref, o_ref):
    for j in range(BT_ATT):
        q = cq_ref[j]
        r = jnp.sum(mk_ref[j] * q[None], axis=-1, keepdims=True)
        r = jnp.broadcast_to(r, (N_MEM, PACK_ROWS, 128))
        s = r + pltpu.roll(r, C_HEADS, 1)
        e = jnp.exp2(s - jnp.max(s, axis=0, keepdims=True))
        o_ref[j] = jnp.sum(e * mv_ref[j], axis=0) / jnp.sum(e, axis=0)


def _sample_back_kernel(x_ref, h_ref, oa_ref, og_ref, szb_ref, oc_ref, szc_ref, gnf_ref,
                        wg_ref, bg_ref, wbr_ref, wout_ref, gf_ref, xo_ref, *, final):
    h = h_ref[...]
    out_b = (_head_rms(og_ref[...], gnf_ref[...]) * szb_ref[...]).astype(BF16)
    oc = jnp.concatenate(
        [oc_ref[:, (c % 2) * C_HEADS + c // 2, :] for c in range(PACK_ROWS)], axis=1)
    out_c = (oc * szc_ref[...]).astype(BF16)
    merged = None
    for i, br in enumerate((oa_ref[...], out_b, out_c)):
        cs = slice(i * D_MODEL, (i + 1) * D_MODEL)
        gate = _sigmoid(_dot(h, wg_ref[:, cs]) + bg_ref[:, cs])
        term = gate * _dot(br, wbr_ref[i])
        merged = term if merged is None else merged + term
    x_new = x_ref[...] + _dot(merged.astype(BF16), wout_ref[...])
    xo_ref[...] = _rms(x_new, gf_ref[...]) if final else x_new


def _sample_back(x, h, oa, og, szb, oc, szc, p, l):
    acts = (x, h, oa, og, szb, oc, szc)
    branch_w = _resident(p["wbr"].shape)
    return pl.pallas_call(
        functools.partial(_sample_back_kernel, final=l == DEPTH - 1),
        grid=(1,),
        in_specs=[_resident(a.shape) for a in acts] + [
            _layer(p["gnf"], l), _layer(p["wg"], l), _layer(p["bg"], l), branch_w,
            _layer(p["wout"], l), _resident(p["gf"].shape)],
        out_specs=pl.BlockSpec(x.shape, lambda i: (0, 0)),
        out_shape=jax.ShapeDtypeStruct(x.shape, F32),
        compiler_params=_params(("arbitrary",)),
        name="sample_back",
    )(*acts, p["gnf"], p["wg"], p["bg"], p["wbr"], p["wout"], p["gf"])


def kernel(x_prompt, x_sample, state_gla, cache_mem_k, cache_mem_v, mem_prompt, g_norm, w_in, b_gate, w_s, b_s, ln_v_g, ln_v_b, w_g_up, b_g, gn_g, g_mem, w_mem_kv, w_branch, w_out, g_final):
    batch, seq, _ = x_prompt.shape
    dec_batch = x_sample.shape[0]
    rows = lambda a: a.reshape(DEPTH, 1, -1)

    w_in_t = jnp.swapaxes(w_in, 1, 2)
    w_split = _split_w_in(w_in_t, 0)
    p = {
        "wgup": jnp.pad(w_g_up, ((0, 0), (0, GATE_RANK_PAD - GATE_RANK), (0, 0))).astype(BF16),
        "gn": rows(g_norm), "lng": rows(ln_v_g), "lnb": rows(ln_v_b), "bgu": rows(b_g),
        "bg": rows(b_gate), "gnf": rows(jnp.tile(gn_g, (1, B_HEADS))),
        "ws": w_s,
        "bs_full": jnp.repeat(jnp.swapaxes(b_s, 1, 2), A_HEAD_DIM, axis=2),
        "ws0": rows(jnp.repeat(w_s[:, :, 0, 0], A_HEAD_DIM, axis=1)),
        "bs0": rows(jnp.repeat(b_s[:, :, 0], A_HEAD_DIM, axis=1)),
        "gf": g_final.reshape(1, D_MODEL),
    }

    mk_out, mv_out, mk_all, mv_all = _memkv(
        mem_prompt, g_mem.reshape(DEPTH, 1, D_MODEL), w_mem_kv)

    xp = x_prompt.reshape(batch * seq, D_MODEL)
    xs = x_sample.reshape(dec_batch, D_MODEL)
    cache_k = _pack_heads(cache_mem_k)
    cache_v = _pack_heads(cache_mem_v)
    gla_p, v_s = [], []
    gla_s = None
    w_branch_rows = w_branch.reshape(DEPTH, N_BRANCH * D_MODEL, D_MODEL)
    w_proj = (_cast_bf16(w_branch.reshape(DEPTH * N_BRANCH, D_MODEL, D_MODEL), N_BRANCH),
              _cast_bf16(w_out, 1))
    for l in range(DEPTH):
        p.update(zip(("wa", "wb", "wc", "wg"), w_split))
        p.update(zip(("wbr", "wout"), w_proj))
        (hs_, vn, oa, qkd, v, szb, cq_packed, szc) = _sample_front(xs, p, l)
        qkd = _stack_gla_rows(qkd)
        sv = v.reshape(dec_batch // BT_GLA, BT_GLA, D_MODEL)

        more = l + 1 < DEPTH
        h, m, gla_s, og, *proj_next = _prompt_a(
            xp, qkd, sv, state_gla, gla_s, p, l,
            w_branch_rows if more else None, w_out if more else None)
        m, st, w_next = _prompt_b(h, m, p, l, batch, w_in_t if more else None)
        xp, oc = _prompt_c(xp, h, m, mk_all, mv_all, cq_packed, cache_k, cache_v, p, l, batch)
        gla_p.append(jnp.swapaxes(st, -1, -2))

        og = og.reshape(dec_batch, D_MODEL)
        xs = _sample_back(xs, hs_, oa, og, szb, oc, szc, p, l)
        v_s.append(vn)
        w_split = w_next
        if more:
            w_proj = (proj_next[0].reshape(N_BRANCH, D_MODEL, D_MODEL), proj_next[1])

    y_prompt = xp.reshape(batch, seq, D_MODEL)
    y_sample = xs.reshape(dec_batch, 1, D_MODEL)
    return (y_prompt, y_sample, jnp.stack(gla_p), mk_out, mv_out, gla_s,
            jnp.stack(v_s).reshape(DEPTH, dec_batch, 1, A_HEADS, A_HEAD_DIM))
```

```python
import functools

import jax
import jax.numpy as jnp
from jax import lax
from jax.experimental import pallas as pl
from jax.experimental.pallas import tpu as pltpu

F32 = jnp.float32
BF16 = jnp.bfloat16

D_MODEL = 1024
DEPTH = 4
CHUNK_A = 128
A_HEADS = 8
A_HEAD_DIM = 128
B_HEADS = 4
B_KEY_DIM = 512
B_HK = 128
B_HV = 256
GATE_RANK = 16
GATE_RANK_PAD = 128
INV_GATE_NORMALIZER = 1.0 / 16.0
CHUNK_B = 64
N_MEM = 256
C_HEADS = 4
C_HD = 256
N_BRANCH = 3
EPS = 1e-6
LOG2_E = 1.4426950408889634
LANES = 128

VMEM_LIMIT_BYTES = 56 * 1024 * 1024
TM = 512
BT_GLA = 4
BT_ATT = 4


def _dot(a, b):
    return jnp.dot(a, b, preferred_element_type=F32)


def _dot_nt(a, b):
    return lax.dot_general(a, b, (((1,), (1,)), ((), ())), preferred_element_type=F32)


def _dot_tn(a, b):
    return lax.dot_general(a, b, (((0,), (0,)), ((), ())), preferred_element_type=F32)


def _sigmoid(x):
    return 1.0 / (1.0 + jnp.exp(-x))


def _silu(x):
    return x * _sigmoid(x)


def _gelu(x):
    c = 0.7978845608028654
    return x * (0.5 * (1.0 + jnp.tanh(c * (x + 0.044715 * (x * x * x)))))


def _log_sigmoid(x):
    return jnp.minimum(x, 0.0) - jnp.log1p(jnp.exp(-jnp.abs(x)))


def _rms(x, g):
    return x * lax.rsqrt(jnp.mean(x * x, axis=-1, keepdims=True) + EPS) * g


def _head_rms(o, g_full):
    parts = []
    for hd in range(B_HEADS):
        oh = o[:, hd * B_HV:(hd + 1) * B_HV]
        parts.append(oh * lax.rsqrt(jnp.mean(oh * oh, axis=-1, keepdims=True) + EPS))
    return jnp.concatenate(parts, axis=1) * g_full


def _layernorm(v, g, b):
    mu = jnp.mean(v, axis=-1, keepdims=True)
    vc = v - mu
    var = jnp.mean(vc * vc, axis=-1, keepdims=True)
    return vc * lax.rsqrt(var + EPS) * g + b


def _resident(shape):
    nd = len(shape)
    return pl.BlockSpec(shape, lambda *_: (0,) * nd, pipeline_mode=pl.Buffered(1))


def _layer(arr, layer, cols=None, col_block=0):
    shape = arr.shape[1:] if cols is None else arr.shape[1:-1] + (cols,)
    idx = (min(layer, arr.shape[0] - 1),) + (0,) * (len(shape) - 1) + (col_block,)
    return pl.BlockSpec((None,) + shape, lambda *_: idx, pipeline_mode=pl.Buffered(1))


def _params(semantics=None):
    return pltpu.CompilerParams(dimension_semantics=semantics,
                                vmem_limit_bytes=VMEM_LIMIT_BYTES)


O_BQ = 3 * D_MODEL
O_GD = O_BQ + 2 * B_KEY_DIM + D_MODEL
O_BZ = O_GD + GATE_RANK
O_CQ = O_BZ + D_MODEL
O_GL = O_CQ + 2 * D_MODEL
IN_COLS = O_GL + 3 * D_MODEL
WB_COLS = 2 * B_KEY_DIM + 2 * D_MODEL + GATE_RANK_PAD
TR_PREP = 128


def _split_w_in_kernel(wt_ref, wa_ref, wb_ref, wc_ref, wg_ref):
    def seg(lo, hi):
        return wt_ref[lo:hi, :].T.astype(BF16)

    wa_ref[...] = seg(0, O_BQ)
    wb_ref[:, :O_GD - O_BQ] = seg(O_BQ, O_GD)
    wb_ref[:, O_GD - O_BQ:O_GD - O_BQ + D_MODEL] = seg(O_BZ, O_CQ)
    gd = wt_ref[O_GD:O_GD + GATE_RANK_PAD, :].T
    lane = lax.broadcasted_iota(jnp.int32, gd.shape, 1)
    wb_ref[:, O_GD - O_BQ + D_MODEL:] = jnp.where(lane < GATE_RANK, gd, 0.0).astype(BF16)
    wc_ref[...] = seg(O_CQ, O_GL)
    wg_ref[...] = seg(O_GL, IN_COLS)


SPLIT_WIDTHS = (O_BQ, WB_COLS, O_GL - O_CQ, IN_COLS - O_GL)


def _split_w_in(w_in_t, layer):
    rows = w_in_t.shape[2]
    return pl.pallas_call(
        _split_w_in_kernel,
        grid=(rows // TR_PREP,),
        in_specs=[pl.BlockSpec((None, IN_COLS, TR_PREP), lambda i: (layer, 0, i))],
        out_specs=[pl.BlockSpec((None, TR_PREP, n), lambda i: (0, i, 0)) for n in SPLIT_WIDTHS],
        out_shape=[jax.ShapeDtypeStruct((1, rows, n), BF16) for n in SPLIT_WIDTHS],
        compiler_params=_params(("parallel",)),
        name="split_w_in",
    )(w_in_t)


def _cast_kernel(w_ref, o_ref):
    o_ref[...] = w_ref[...].astype(BF16)


def _cast_bf16(w, count):
    _, rows, cols = w.shape
    spec = pl.BlockSpec((None, rows, cols), lambda i: (i, 0, 0))
    return pl.pallas_call(
        _cast_kernel,
        grid=(count,),
        in_specs=[spec],
        out_specs=spec,
        out_shape=jax.ShapeDtypeStruct((count, rows, cols), BF16),
        compiler_params=_params(("parallel",)),
        name="cast_bf16",
    )(w)


MEM_BATCHES = 2


def _memkv_kernel(mem_ref, g_ref, w_ref, mk_ref, mv_ref, mk16_ref, mv16_ref, w16_scr):
    @pl.when(pl.program_id(1) == 0)
    def _():
        w16_scr[...] = w_ref[...].astype(BF16)

    mem = mem_ref[...].reshape(MEM_BATCHES * N_MEM, D_MODEL)
    hn = _rms(mem, g_ref[...]).astype(BF16)
    mk = _dot(hn, w16_scr[:, :D_MODEL])
    mv = _dot(hn, w16_scr[:, D_MODEL:])
    for b in range(MEM_BATCHES):
        rows = slice(b * N_MEM, (b + 1) * N_MEM)
        for hd in range(C_HEADS):
            hs = slice(hd * C_HD, (hd + 1) * C_HD)
            mk_ref[b, :, hd, :] = mk[rows, hs]
            mv_ref[b, :, hd, :] = mv[rows, hs]
    mk16_ref[...] = mk.astype(BF16)
    mv16_ref[...] = mv.astype(BF16)


def _memkv(mem, g_mem, w_kv):
    batch = mem.shape[0]
    rows = MEM_BATCHES * N_MEM
    out5 = jax.ShapeDtypeStruct((DEPTH, batch, N_MEM, C_HEADS, C_HD), F32)
    out16 = jax.ShapeDtypeStruct((DEPTH, batch * N_MEM, D_MODEL), BF16)
    spec5 = pl.BlockSpec((None, MEM_BATCHES, N_MEM, C_HEADS, C_HD),
                         lambda l, b: (l, b, 0, 0, 0))
    spec16 = pl.BlockSpec((None, rows, D_MODEL), lambda l, b: (l, b, 0))
    return pl.pallas_call(
        _memkv_kernel,
        grid=(DEPTH, batch // MEM_BATCHES),
        in_specs=[
            pl.BlockSpec((MEM_BATCHES, N_MEM, D_MODEL), lambda l, b: (b, 0, 0)),
            pl.BlockSpec((None, 1, D_MODEL), lambda l, b: (l, 0, 0)),
            pl.BlockSpec((None, D_MODEL, 2 * D_MODEL), lambda l, b: (l, 0, 0)),
        ],
        out_specs=[spec5, spec5, spec16, spec16],
        out_shape=[out5, out5, out16, out16],
        scratch_shapes=[pltpu.VMEM((D_MODEL, 2 * D_MODEL), BF16)],
        compiler_params=_params(("arbitrary", "arbitrary")),
        name="memkv",
    )(mem, g_mem, w_kv)


def _prompt_a_kernel(x_ref, gn_ref, wa_ref, lng_ref, lnb_ref, ws_ref, bs_ref,
                     wg_ref, bg_ref, wbr_ref, qkd_ref, sv_ref, s0_ref, *rest,
                     aliased, cast_rows):
    rest = list(rest)
    s_scr = rest.pop()
    if cast_rows:
        wout_next, wbr_next = rest.pop(), rest.pop()
    so_ref, sn_ref, m_ref, h_ref = rest.pop(), rest.pop(), rest.pop(), rest.pop()
    if aliased:
        rest.pop()
    if cast_rows:
        wout_src, wbr_src = rest.pop(), rest.pop()

        @pl.when(pl.program_id(0) < cast_rows)
        def _():
            wbr_next[...] = wbr_src[...].astype(BF16)

        @pl.when(pl.program_id(0) >= cast_rows)
        def _():
            wout_next[...] = wout_src[...].astype(BF16)

    _sample_gla_tile(qkd_ref, sv_ref, s0_ref, sn_ref, so_ref)
    n_chunks = TM // CHUNK_A
    h = _rms(x_ref[...], gn_ref[...]).astype(BF16)
    h_ref[...] = h
    v = _gelu(_dot(h, wa_ref[:, D_MODEL:2 * D_MODEL]))
    vn = _layernorm(v, lng_ref[...], lnb_ref[...]).astype(BF16)
    row = lax.broadcasted_iota(jnp.int32, (CHUNK_A, CHUNK_A), 0)
    col = lax.broadcasted_iota(jnp.int32, (CHUNK_A, CHUNK_A), 1)
    causal = row >= col
    for hd in range(A_HEADS):
        cs = slice(hd * A_HEAD_DIM, (hd + 1) * A_HEAD_DIM)
        w = jnp.where(causal, ws_ref[hd], 0.0).astype(BF16)
        vh = jnp.concatenate(
            [vn[c * CHUNK_A:(c + 1) * CHUNK_A, cs] for c in range(n_chunks)], axis=1)
        mixed = _dot(w, vh)
        for c in range(n_chunks):
            s_scr[c * CHUNK_A:(c + 1) * CHUNK_A, cs] = mixed[:, c * CHUNK_A:(c + 1) * CHUNK_A]
    u = _gelu(_dot(h, wa_ref[:, :D_MODEL]))
    z = _dot(h, wa_ref[:, 2 * D_MODEL:])
    bias = jnp.concatenate([bs_ref[...]] * n_chunks, axis=0)
    out_a = (u * (s_scr[...] + bias) * _silu(z)).astype(BF16)
    gate = _sigmoid(_dot(h, wg_ref[...]) + bg_ref[...])
    m_ref[...] = gate * _dot(out_a, wbr_ref[...])


def _prompt_a(x, qkd, sv, s0_all, s_all, p, l, w_branch=None, w_out=None):
    n = x.shape[0]
    steps = n // TM
    assert qkd.shape[0] == steps, "one sample batch tile per prompt grid step"
    tile = lambda i: (i, 0)
    first = s_all is None
    n_slabs = s0_all.shape[0] if first else 1
    assert not first or l == 0
    state_blk = (BT_GLA, B_HEADS, B_HK, B_HV)
    args = [x, p["gn"], p["wa"], p["lng"], p["lnb"], p["ws"], p["bs_full"],
            p["wg"], p["bg"], p["wbr"], qkd, sv, s0_all]
    in_specs = [
        pl.BlockSpec((TM, D_MODEL), tile),
        _layer(p["gn"], l), _layer(p["wa"], l), _layer(p["lng"], l),
        _layer(p["lnb"], l), _layer(p["ws"], l), _layer(p["bs_full"], l),
        _layer(p["wg"], l, D_MODEL, 0), _layer(p["bg"], l, D_MODEL, 0),
        _layer(p["wbr"], 0),
        pl.BlockSpec((None, GLA_ROWS, B_HK), lambda i: (i, 0, 0)),
        pl.BlockSpec((None, BT_GLA, D_MODEL), lambda i: (i, 0, 0)),
        pl.BlockSpec((None,) + state_blk, lambda i: (l, i, 0, 0, 0)),
    ]
    out_specs = [pl.BlockSpec((TM, D_MODEL), tile), pl.BlockSpec((TM, D_MODEL), tile),
                 pl.BlockSpec((n_slabs,) + state_blk, lambda i: (l, i, 0, 0, 0)),
                 pl.BlockSpec((None, BT_GLA, D_MODEL), lambda i: (i, 0, 0))]
    out_shape = [jax.ShapeDtypeStruct((n, D_MODEL), BF16),
                 jax.ShapeDtypeStruct((n, D_MODEL), F32),
                 jax.ShapeDtypeStruct(s0_all.shape, F32),
                 jax.ShapeDtypeStruct(sv.shape, F32)]
    cast_rows = 0
    if w_branch is not None:
        blk = (w_branch.shape[1] + w_out.shape[1]) // steps
        cast_rows = w_branch.shape[1] // blk
        assert blk * steps == w_branch.shape[1] + w_out.shape[1] and blk % 16 == 0
        br_blk = lambda i: jnp.minimum(i, cast_rows - 1)
        out_blk = lambda i: jnp.maximum(i - cast_rows, 0)
        args += [w_branch, w_out]
        in_specs += [pl.BlockSpec((None, blk, D_MODEL), lambda i: (l + 1, br_blk(i), 0)),
                     pl.BlockSpec((None, blk, D_MODEL), lambda i: (l + 1, out_blk(i), 0))]
        out_specs += [pl.BlockSpec((blk, D_MODEL), lambda i: (br_blk(i), 0)),
                      pl.BlockSpec((None, blk, D_MODEL), lambda i: (0, out_blk(i), 0))]
        out_shape += [jax.ShapeDtypeStruct(w_branch.shape[1:], BF16),
                      jax.ShapeDtypeStruct((1,) + w_out.shape[1:], BF16)]
    aliases = {}
    if not first:
        in_specs.append(pl.BlockSpec(memory_space=pl.ANY))
        args.append(s_all)
        aliases = {len(args) - 1: 2}
    return pl.pallas_call(
        functools.partial(_prompt_a_kernel, aliased=not first, cast_rows=cast_rows),
        grid=(steps,),
        in_specs=in_specs,
        out_specs=out_specs,
        out_shape=out_shape,
        scratch_shapes=[pltpu.VMEM((TM, D_MODEL), F32)],
        input_output_aliases=aliases,
        compiler_params=_params(("arbitrary",)),
        name="prompt_a",
    )(*args)


def _prompt_b_kernel(h_ref, ma_ref, wb_ref, wgup_ref, bgu_ref, gnf_ref,
                     wg_ref, bg_ref, wbr_ref, *rest, split_every):
    if split_every:
        wt_ref, m_ref, st_ref, *split_out, o_scr = rest
        step = pl.program_id(0) * pl.num_programs(1) + pl.program_id(1)

        @pl.when(step % split_every == 0)
        def _():
            _split_w_in_kernel(wt_ref, *split_out)
    else:
        m_ref, st_ref, o_scr = rest

    @pl.when(pl.program_id(1) == 0)
    def _():
        st_ref[...] = jnp.zeros_like(st_ref)

    h = h_ref[...]
    q = _dot(h, wb_ref[:, 0:B_KEY_DIM]) * (B_HK ** -0.5)
    k = _dot(h, wb_ref[:, B_KEY_DIM:2 * B_KEY_DIM])
    v = _dot(h, wb_ref[:, 2 * B_KEY_DIM:2 * B_KEY_DIM + D_MODEL])
    gd = _dot(h, wb_ref[:, 2 * B_KEY_DIM + 2 * D_MODEL:]).astype(BF16)
    logit = _dot(gd, wgup_ref[...]) + bgu_ref[...]
    z_off = 2 * B_KEY_DIM + D_MODEL
    z = _dot(h, wb_ref[:, z_off:z_off + D_MODEL])
    gate_logit = _dot(h, wg_ref[...]) + bg_ref[...]

    n_chunks = TM // CHUNK_B
    chunk_rows = [slice(c * CHUNK_B, (c + 1) * CHUNK_B) for c in range(n_chunks)]
    row = lax.broadcasted_iota(jnp.int32, (CHUNK_B, CHUNK_B), 0)
    col = lax.broadcasted_iota(jnp.int32, (CHUNK_B, CHUNK_B), 1)
    causal = row >= col
    tri = jnp.where(causal, 1.0, 0.0).astype(BF16)

    g = _log_sigmoid(logit) * INV_GATE_NORMALIZER
    g_hi = g.astype(BF16)
    g_lo = (g - g_hi.astype(F32)).astype(BF16)
    bcum = jnp.concatenate(
        [_dot(tri, g_hi[rs]) + _dot(tri, g_lo[rs]) for rs in chunk_rows], axis=0)
    bcum3 = bcum.reshape(n_chunks, CHUNK_B, B_KEY_DIM)
    b_last3 = bcum3[:, CHUNK_B - 1:CHUNK_B, :]
    q_dec = (q * jnp.exp(bcum)).astype(BF16)
    k_dec = (k * jnp.exp(-bcum)).astype(BF16)
    k_tail = (k.reshape(bcum3.shape) * jnp.exp(b_last3 - bcum3)).astype(BF16).reshape(k.shape)
    decay3 = jnp.exp(b_last3)
    vb = v.astype(BF16)

    scores, kv_t = {}, {}
    for c, rs in enumerate(chunk_rows):
        for hd in range(B_HEADS):
            ks = slice(hd * B_HK, (hd + 1) * B_HK)
            vs = slice(hd * B_HV, (hd + 1) * B_HV)
            scores[c, hd] = jnp.where(
                causal, _dot_nt(q_dec[rs, ks], k_dec[rs, ks]), 0.0).astype(BF16)
            kv_t[c, hd] = _dot_tn(vb[rs, vs], k_tail[rs, ks])
    silu_z = _silu(z)
    gate = _sigmoid(gate_logit)

    for hd in range(B_HEADS):
        ks = slice(hd * B_HK, (hd + 1) * B_HK)
        vs = slice(hd * B_HV, (hd + 1) * B_HV)
        state_t = st_ref[hd]
        for c, rs in enumerate(chunk_rows):
            o_scr[rs, vs] = (_dot(scores[c, hd], vb[rs, vs])
                             + _dot_nt(q_dec[rs, ks], state_t.astype(BF16)))
            state_t = state_t * decay3[c][:, ks] + kv_t[c, hd]
        st_ref[hd] = state_t

    out_b = (_head_rms(o_scr[...], gnf_ref[...]) * silu_z).astype(BF16)
    m_ref[...] = ma_ref[...] + gate * _dot(out_b, wbr_ref[...])


def _prompt_b(h, ma, p, l, batch, w_in_t=None):
    n = h.shape[0]
    steps = n // batch // TM
    tile = lambda b, t: (b * steps + t, 0)
    args = [h, ma, p["wb"], p["wgup"], p["bgu"], p["gnf"], p["wg"], p["bg"], p["wbr"]]
    in_specs = [
        pl.BlockSpec((TM, D_MODEL), tile), pl.BlockSpec((TM, D_MODEL), tile),
        _layer(p["wb"], l), _layer(p["wgup"], l), _layer(p["bgu"], l),
        _layer(p["gnf"], l),
        _layer(p["wg"], l, D_MODEL, 1), _layer(p["bg"], l, D_MODEL, 1),
        _layer(p["wbr"], 1),
    ]
    out_specs = [
        pl.BlockSpec((TM, D_MODEL), tile),
        pl.BlockSpec((None, B_HEADS, B_HV, B_HK), lambda b, t: (b, 0, 0, 0)),
    ]
    out_shape = [jax.ShapeDtypeStruct((n, D_MODEL), F32),
                 jax.ShapeDtypeStruct((batch, B_HEADS, B_HV, B_HK), F32)]
    split_every = 0
    if w_in_t is not None:
        slabs = w_in_t.shape[2] // TR_PREP
        split_every = batch * steps // slabs
        assert split_every * slabs == batch * steps
        slab = lambda b, t: (b * steps + t) // split_every
        args.append(w_in_t)
        in_specs.append(pl.BlockSpec((None, IN_COLS, TR_PREP), lambda b, t: (l + 1, 0, slab(b, t))))
        out_specs += [pl.BlockSpec((None, TR_PREP, c), lambda b, t: (0, slab(b, t), 0))
                      for c in SPLIT_WIDTHS]
        out_shape += [jax.ShapeDtypeStruct((1, w_in_t.shape[2], c), BF16) for c in SPLIT_WIDTHS]
    res = pl.pallas_call(
        functools.partial(_prompt_b_kernel, split_every=split_every),
        grid=(batch, steps),
        in_specs=in_specs,
        out_specs=out_specs,
        out_shape=out_shape,
        scratch_shapes=[pltpu.VMEM((TM, D_MODEL), F32)],
        compiler_params=_params(("arbitrary", "arbitrary")),
        name="prompt_b",
    )(*args)
    return res[0], res[1], tuple(res[2:])


def _prompt_c_kernel(x_ref, h_ref, mab_ref, mk_ref, mv_ref, wc_ref,
                     wg_ref, bg_ref, wbr_ref, wout_ref, gf_ref, cq_ref, ck_ref, cv_ref,
                     xo_ref, oc_ref, *, final):
    h = h_ref[...]
    q = _dot(h, wc_ref[:, :D_MODEL]).astype(BF16)
    z = _dot(h, wc_ref[:, D_MODEL:])
    gate_logit = _dot(h, wg_ref[...]) + bg_ref[...]
    mk = mk_ref[...]
    mv = mv_ref[...]
    parts = []
    for hd in range(C_HEADS):
        hs = slice(hd * C_HD, (hd + 1) * C_HD)
        s = _dot_nt(q[:, hs], mk[:, hs]) * (C_HD ** -0.5)
        e = jnp.exp(s - jnp.max(s, axis=-1, keepdims=True))
        p = e / jnp.sum(e, axis=-1, keepdims=True)
        parts.append(_dot(p.astype(BF16), mv[:, hs]))
    out_c = (jnp.concatenate(parts, axis=1) * _silu(z)).astype(BF16)
    merged = mab_ref[...] + _sigmoid(gate_logit) * _dot(out_c, wbr_ref[...])
    x_new = x_ref[...] + _dot(merged.astype(BF16), wout_ref[...])
    xo_ref[...] = _rms(x_new, gf_ref[...]) if final else x_new
    _sample_att_tile(cq_ref, ck_ref, cv_ref, oc_ref)


def _prompt_c(x, h, mab, mk_all, mv_all, cq_packed, cache_k, cache_v, p, l, batch):
    n = x.shape[0]
    steps = n // batch // TM
    dec_batch = cq_packed.shape[0]
    assert dec_batch == batch * steps * BT_ATT, "one sample batch tile per prompt grid step"
    tile = lambda b, t: (b * steps + t, 0)
    mem = lambda b, t: (l, b, 0)
    att_row = lambda b, t: (b * steps + t, 0, 0)
    att_cache = lambda b, t: (l, b * steps + t, 0, 0, 0)
    args = (x, h, mab, mk_all, mv_all, p["wc"], p["wg"], p["bg"], p["wbr"], p["wout"], p["gf"],
            cq_packed, cache_k, cache_v)
    return pl.pallas_call(
        functools.partial(_prompt_c_kernel, final=l == DEPTH - 1),
        grid=(batch, steps),
        in_specs=[
            pl.BlockSpec((TM, D_MODEL), tile), pl.BlockSpec((TM, D_MODEL), tile),
            pl.BlockSpec((TM, D_MODEL), tile),
            pl.BlockSpec((None, N_MEM, D_MODEL), mem), pl.BlockSpec((None, N_MEM, D_MODEL), mem),
            _layer(p["wc"], l),
            _layer(p["wg"], l, D_MODEL, 2), _layer(p["bg"], l, D_MODEL, 2),
            _layer(p["wbr"], 2), _layer(p["wout"], l),
            _resident(p["gf"].shape),
            pl.BlockSpec((BT_ATT, PACK_ROWS, LANES), att_row),
            pl.BlockSpec((None, BT_ATT, N_MEM, PACK_ROWS, LANES), att_cache),
            pl.BlockSpec((None, BT_ATT, N_MEM, PACK_ROWS, LANES), att_cache),
        ],
        out_specs=[pl.BlockSpec((TM, D_MODEL), tile),
                   pl.BlockSpec((BT_ATT, PACK_ROWS, LANES), att_row)],
        out_shape=[jax.ShapeDtypeStruct((n, D_MODEL), F32),
                   jax.ShapeDtypeStruct((dec_batch, PACK_ROWS, LANES), F32)],
        compiler_params=_params(("parallel", "arbitrary")),
        name="prompt_c",
    )(*args)


def _sample_front_kernel(x_ref, gn_ref, wa_ref, wb_ref, wc_ref, lng_ref, lnb_ref,
                         ws0_ref, bs0_ref, wgup_ref, bgu_ref,
                         h_ref, vn_ref, oa_ref, qkd_ref, v_ref, szb_ref, cqp_ref, szc_ref):
    h = _rms(x_ref[...], gn_ref[...]).astype(BF16)
    h_ref[...] = h
    u = _gelu(_dot(h, wa_ref[:, :D_MODEL]))
    vn = _layernorm(_gelu(_dot(h, wa_ref[:, D_MODEL:2 * D_MODEL])), lng_ref[...], lnb_ref[...])
    vn_ref[...] = vn
    z = _dot(h, wa_ref[:, 2 * D_MODEL:])
    oa_ref[...] = (u * (vn * ws0_ref[...] + bs0_ref[...]) * _silu(z)).astype(BF16)
    q = _dot(h, wb_ref[:, 0:B_KEY_DIM]) * (B_HK ** -0.5)
    k = _dot(h, wb_ref[:, B_KEY_DIM:2 * B_KEY_DIM])
    v_ref[...] = _dot(h, wb_ref[:, 2 * B_KEY_DIM:2 * B_KEY_DIM + D_MODEL])
    szb_ref[...] = _silu(_dot(h, wb_ref[:, 2 * B_KEY_DIM + D_MODEL:2 * B_KEY_DIM + 2 * D_MODEL]))
    gd = _dot(h, wb_ref[:, 2 * B_KEY_DIM + 2 * D_MODEL:]).astype(BF16)
    g = _log_sigmoid(_dot(gd, wgup_ref[...]) + bgu_ref[...]) * INV_GATE_NORMALIZER
    for a, arr in enumerate((q, k, jnp.exp(g))):
        for hd in range(B_HEADS):
            qkd_ref[:, a * B_HEADS + hd, :] = arr[:, hd * B_HK:(hd + 1) * B_HK]
    cq = _dot(h, wc_ref[:, :D_MODEL]) * (LOG2_E * C_HD ** -0.5)
    for r in range(PACK_ROWS):
        lo = (r % C_HEADS) * C_HD + (r // C_HEADS) * LANES
        cqp_ref[:, r, :] = cq[:, lo:lo + LANES]
    szc_ref[...] = _silu(_dot(h, wc_ref[:, D_MODEL:]))


def _sample_front(x, p, l):
    n = x.shape[0]
    wide = jax.ShapeDtypeStruct((n, D_MODEL), F32)
    wide16 = jax.ShapeDtypeStruct((n, D_MODEL), BF16)
    qkd = jax.ShapeDtypeStruct((n, 3 * B_HEADS, B_HK), F32)
    cqp = jax.ShapeDtypeStruct((n, PACK_ROWS, LANES), F32)
    out_shape = [wide16, wide, wide16, qkd, wide, wide, cqp, wide]
    names = ("gn", "wa", "wb", "wc", "lng", "lnb", "ws0", "bs0", "wgup", "bgu")
    return pl.pallas_call(
        _sample_front_kernel,
        grid=(1,),
        in_specs=[_resident(x.shape)] + [_layer(p[k], l) for k in names],
        out_specs=[pl.BlockSpec(s.shape, lambda i, nd=len(s.shape): (0,) * nd)
                   for s in out_shape],
        out_shape=out_shape,
        compiler_params=_params(("arbitrary",)),
        name="sample_front",
    )(x, *(p[k] for k in names))


GLA_ROWS = 3 * B_HEADS * BT_GLA


def _sample_gla_tile(qkd_ref, v_ref, s0_ref, s_ref, o_ref):
    pad = jnp.zeros((B_HK - GLA_ROWS, B_HK), F32)
    cols = jnp.concatenate([qkd_ref[...], pad], axis=0).T
    for j in range(BT_GLA):
        for hd in range(B_HEADS):
            vs = slice(hd * B_HV, (hd + 1) * B_HV)
            iq = hd * BT_GLA + j
            ik = (B_HEADS + hd) * BT_GLA + j
            ie = (2 * B_HEADS + hd) * BT_GLA + j
            state = (cols[:, ie:ie + 1] * s0_ref[j, hd]
                     + cols[:, ik:ik + 1] * v_ref[j:j + 1, vs])
            s_ref[0, j, hd] = state
            o_ref[j:j + 1, vs] = jnp.sum(cols[:, iq:iq + 1] * state, axis=0, keepdims=True)
    if s_ref.shape[0] > 1:
        s_ref[1:] = jnp.zeros((s_ref.shape[0] - 1,) + s_ref.shape[1:], F32)


def _stack_gla_rows(qkd):
    n, groups, _ = qkd.shape
    a = qkd.reshape(n // BT_GLA, BT_GLA, groups, B_HK)
    return jnp.swapaxes(a, 1, 2).reshape(n // BT_GLA, GLA_ROWS, B_HK)


PACK_ROWS = C_HEADS * C_HD // LANES


def _pack_heads(a):
    lead = a.shape[:-2]
    a = a.reshape(*lead, C_HEADS, C_HD // LANES, LANES)
    return jnp.swapaxes(a, -2, -3).reshape(*lead, PACK_ROWS, LANES)


def _sample_att_tile(cq_ref, mk_ref, mv_ref, o_ref):
    for j in range(BT_ATT):
        q = cq_ref[j]
        r = jnp.sum(mk_ref[j] * q[None], axis=-1, keepdims=True)
        r = jnp.broadcast_to(r, (N_MEM, PACK_ROWS, LANES))
        s = r + pltpu.roll(r, C_HEADS, 1)
        e = jnp.exp2(s - jnp.max(s, axis=0, keepdims=True))
        o_ref[j] = jnp.sum(e * mv_ref[j], axis=0) / jnp.sum(e, axis=0)


def _sample_back_kernel(x_ref, h_ref, oa_ref, og_ref, szb_ref, oc_ref, szc_ref, gnf_ref,
                        wg_ref, bg_ref, wbr_ref, wout_ref, gf_ref, xo_ref, *, final):
    h = h_ref[...]
    out_b = (_head_rms(og_ref[...], gnf_ref[...]) * szb_ref[...]).astype(BF16)
    oc = jnp.concatenate(
        [oc_ref[:, (c % 2) * C_HEADS + c // 2, :] for c in range(PACK_ROWS)], axis=1)
    out_c = (oc * szc_ref[...]).astype(BF16)
    merged = None
    for i, br in enumerate((oa_ref[...], out_b, out_c)):
        cs = slice(i * D_MODEL, (i + 1) * D_MODEL)
        gate = _sigmoid(_dot(h, wg_ref[:, cs]) + bg_ref[:, cs])
        term = gate * _dot(br, wbr_ref[i])
        merged = term if merged is None else merged + term
    x_new = x_ref[...] + _dot(merged.astype(BF16), wout_ref[...])
    xo_ref[...] = _rms(x_new, gf_ref[...]) if final else x_new


def _sample_back(x, h, oa, og, szb, oc, szc, p, l):
    acts = (x, h, oa, og, szb, oc, szc)
    branch_w = _resident(p["wbr"].shape)
    return pl.pallas_call(
        functools.partial(_sample_back_kernel, final=l == DEPTH - 1),
        grid=(1,),
        in_specs=[_resident(a.shape) for a in acts] + [
            _layer(p["gnf"], l), _layer(p["wg"], l), _layer(p["bg"], l), branch_w,
            _layer(p["wout"], l), _resident(p["gf"].shape)],
        out_specs=pl.BlockSpec(x.shape, lambda i: (0, 0)),
        out_shape=jax.ShapeDtypeStruct(x.shape, F32),
        compiler_params=_params(("arbitrary",)),
        name="sample_back",
    )(*acts, p["gnf"], p["wg"], p["bg"], p["wbr"], p["wout"], p["gf"])


def kernel(x_prompt, x_sample, state_gla, cache_mem_k, cache_mem_v, mem_prompt, g_norm, w_in, b_gate, w_s, b_s, ln_v_g, ln_v_b, w_g_up, b_g, gn_g, g_mem, w_mem_kv, w_branch, w_out, g_final):
    batch, seq, _ = x_prompt.shape
    dec_batch = x_sample.shape[0]
    rows = lambda a: a.reshape(DEPTH, 1, -1)

    w_in_t = jnp.swapaxes(w_in, 1, 2)
    w_split = _split_w_in(w_in_t, 0)
    p = {
        "wgup": jnp.pad(w_g_up, ((0, 0), (0, GATE_RANK_PAD - GATE_RANK), (0, 0))).astype(BF16),
        "gn": rows(g_norm), "lng": rows(ln_v_g), "lnb": rows(ln_v_b), "bgu": rows(b_g),
        "bg": rows(b_gate), "gnf": rows(jnp.tile(gn_g, (1, B_HEADS))),
        "ws": w_s,
        "bs_full": jnp.repeat(jnp.swapaxes(b_s, 1, 2), A_HEAD_DIM, axis=2),
        "ws0": rows(jnp.repeat(w_s[:, :, 0, 0], A_HEAD_DIM, axis=1)),
        "bs0": rows(jnp.repeat(b_s[:, :, 0], A_HEAD_DIM, axis=1)),
        "gf": g_final.reshape(1, D_MODEL),
    }

    mk_out, mv_out, mk_all, mv_all = _memkv(
        mem_prompt, g_mem.reshape(DEPTH, 1, D_MODEL), w_mem_kv)

    xp = x_prompt.reshape(batch * seq, D_MODEL)
    xs = x_sample.reshape(dec_batch, D_MODEL)
    cache_k = _pack_heads(cache_mem_k)
    cache_v = _pack_heads(cache_mem_v)
    gla_p, v_s = [], []
    gla_s = None
    w_branch_rows = w_branch.reshape(DEPTH, N_BRANCH * D_MODEL, D_MODEL)
    w_proj = (_cast_bf16(w_branch.reshape(DEPTH * N_BRANCH, D_MODEL, D_MODEL), N_BRANCH),
              _cast_bf16(w_out, 1))
    for l in range(DEPTH):
        p.update(zip(("wa", "wb", "wc", "wg"), w_split))
        p.update(zip(("wbr", "wout"), w_proj))
        (hs_, vn, oa, qkd, v, szb, cq_packed, szc) = _sample_front(xs, p, l)
        qkd = _stack_gla_rows(qkd)
        sv = v.reshape(dec_batch // BT_GLA, BT_GLA, D_MODEL)

        more = l + 1 < DEPTH
        h, m, gla_s, og, *proj_next = _prompt_a(
            xp, qkd, sv, state_gla, gla_s, p, l,
            w_branch_rows if more else None, w_out if more else None)
        m, st, w_next = _prompt_b(h, m, p, l, batch, w_in_t if more else None)
        xp, oc = _prompt_c(xp, h, m, mk_all, mv_all, cq_packed, cache_k, cache_v, p, l, batch)
        gla_p.append(jnp.swapaxes(st, -1, -2))

        og = og.reshape(dec_batch, D_MODEL)
        xs = _sample_back(xs, hs_, oa, og, szb, oc, szc, p, l)
        v_s.append(vn)
        w_split = w_next
        if more:
            w_proj = (proj_next[0].reshape(N_BRANCH, D_MODEL, D_MODEL), proj_next[1])

    y_prompt = xp.reshape(batch, seq, D_MODEL)
    y_sample = xs.reshape(dec_batch, 1, D_MODEL)
    return (y_prompt, y_sample, jnp.stack(gla_p), mk_out, mv_out, gla_s,
            jnp.stack(v_s).reshape(DEPTH, dec_batch, 1, A_HEADS, A_HEAD_DIM))
```

```python
import functools

import jax
import jax.numpy as jnp
from jax import lax
from jax.experimental import pallas as pl
from jax.experimental.pallas import tpu as pltpu

F32 = jnp.float32
BF16 = jnp.bfloat16

D_MODEL = 1024
DEPTH = 4
CHUNK_A = 128
A_HEADS = 8
A_HEAD_DIM = 128
B_HEADS = 4
B_KEY_DIM = 512
B_HK = 128
B_HV = 256
GATE_RANK = 16
GATE_RANK_PAD = 128
INV_GATE_NORMALIZER = 1.0 / 16.0
CHUNK_B = 64
N_MEM = 256
C_HEADS = 4
C_HD = 256
N_BRANCH = 3
EPS = 1e-6
LOG2_E = 1.4426950408889634
LANES = 128

VMEM_LIMIT_BYTES = 56 * 1024 * 1024
TM = 512
BT_GLA = 4
BT_ATT = 4


def _dot(a, b):
    return jnp.dot(a, b, preferred_element_type=F32)


def _dot_nt(a, b):
    return lax.dot_general(a, b, (((1,), (1,)), ((), ())), preferred_element_type=F32)


def _dot_tn(a, b):
    return lax.dot_general(a, b, (((0,), (0,)), ((), ())), preferred_element_type=F32)


def _sigmoid(x):
    return 1.0 / (1.0 + jnp.exp(-x))


def _silu(x):
    return x * _sigmoid(x)


def _gelu(x):
    c = 0.7978845608028654
    return x * (0.5 * (1.0 + jnp.tanh(c * (x + 0.044715 * (x * x * x)))))


def _log_sigmoid(x):
    return jnp.minimum(x, 0.0) - jnp.log1p(jnp.exp(-jnp.abs(x)))


def _rms(x, g):
    return x * lax.rsqrt(jnp.mean(x * x, axis=-1, keepdims=True) + EPS) * g


def _head_rms(o, g_full):
    parts = []
    for hd in range(B_HEADS):
        oh = o[:, hd * B_HV:(hd + 1) * B_HV]
        parts.append(oh * lax.rsqrt(jnp.mean(oh * oh, axis=-1, keepdims=True) + EPS))
    return jnp.concatenate(parts, axis=1) * g_full


def _layernorm(v, g, b):
    mu = jnp.mean(v, axis=-1, keepdims=True)
    vc = v - mu
    var = jnp.mean(vc * vc, axis=-1, keepdims=True)
    return vc * lax.rsqrt(var + EPS) * g + b


def _resident(shape):
    nd = len(shape)
    return pl.BlockSpec(shape, lambda *_: (0,) * nd, pipeline_mode=pl.Buffered(1))


def _layer(arr, layer, cols=None, col_block=0):
    shape = arr.shape[1:] if cols is None else arr.shape[1:-1] + (cols,)
    idx = (min(layer, arr.shape[0] - 1),) + (0,) * (len(shape) - 1) + (col_block,)
    return pl.BlockSpec((None,) + shape, lambda *_: idx, pipeline_mode=pl.Buffered(1))


def _params(semantics=None):
    return pltpu.CompilerParams(dimension_semantics=semantics,
                                vmem_limit_bytes=VMEM_LIMIT_BYTES)


O_BQ = 3 * D_MODEL
O_GD = O_BQ + 2 * B_KEY_DIM + D_MODEL
O_BZ = O_GD + GATE_RANK
O_CQ = O_BZ + D_MODEL
O_GL = O_CQ + 2 * D_MODEL
IN_COLS = O_GL + 3 * D_MODEL
WB_COLS = 2 * B_KEY_DIM + 2 * D_MODEL + GATE_RANK_PAD
TR_PREP = 128


def _split_w_in_kernel(wt_ref, wa_ref, wb_ref, wc_ref, wg_ref):
    def seg(lo, hi):
        return wt_ref[lo:hi, :].T.astype(BF16)

    wa_ref[...] = seg(0, O_BQ)
    wb_ref[:, :O_GD - O_BQ] = seg(O_BQ, O_GD)
    wb_ref[:, O_GD - O_BQ:O_GD - O_BQ + D_MODEL] = seg(O_BZ, O_CQ)
    gd = wt_ref[O_GD:O_GD + GATE_RANK_PAD, :].T
    lane = lax.broadcasted_iota(jnp.int32, gd.shape, 1)
    wb_ref[:, O_GD - O_BQ + D_MODEL:] = jnp.where(lane < GATE_RANK, gd, 0.0).astype(BF16)
    wc_ref[...] = seg(O_CQ, O_GL)
    wg_ref[...] = seg(O_GL, IN_COLS)


SPLIT_WIDTHS = (O_BQ, WB_COLS, O_GL - O_CQ, IN_COLS - O_GL)


def _split_w_in(w_in_t, layer):
    rows = w_in_t.shape[2]
    return pl.pallas_call(
        _split_w_in_kernel,
        grid=(rows // TR_PREP,),
        in_specs=[pl.BlockSpec((None, IN_COLS, TR_PREP), lambda i: (layer, 0, i))],
        out_specs=[pl.BlockSpec((None, TR_PREP, n), lambda i: (0, i, 0)) for n in SPLIT_WIDTHS],
        out_shape=[jax.ShapeDtypeStruct((1, rows, n), BF16) for n in SPLIT_WIDTHS],
        compiler_params=_params(("parallel",)),
        name="split_w_in",
    )(w_in_t)


def _cast_kernel(w_ref, o_ref):
    o_ref[...] = w_ref[...].astype(BF16)


def _cast_bf16(w, count):
    _, rows, cols = w.shape
    spec = pl.BlockSpec((None, rows, cols), lambda i: (i, 0, 0))
    return pl.pallas_call(
        _cast_kernel,
        grid=(count,),
        in_specs=[spec],
        out_specs=spec,
        out_shape=jax.ShapeDtypeStruct((count, rows, cols), BF16),
        compiler_params=_params(("parallel",)),
        name="cast_bf16",
    )(w)


MEM_BATCHES = 2


def _memkv_kernel(mem_ref, g_ref, w_ref, mk_ref, mv_ref, mk16_ref, mv16_ref, w16_scr):
    layer = pl.program_id(1)

    @pl.when(pl.program_id(0) == 0)
    def _():
        w16_scr[layer] = w_ref[...].astype(BF16)

    mem = mem_ref[...].reshape(MEM_BATCHES * N_MEM, D_MODEL)
    hn = _rms(mem, g_ref[...]).astype(BF16)
    mk = _dot(hn, w16_scr[layer, :, :D_MODEL])
    mv = _dot(hn, w16_scr[layer, :, D_MODEL:])
    for b in range(MEM_BATCHES):
        rows = slice(b * N_MEM, (b + 1) * N_MEM)
        for hd in range(C_HEADS):
            hs = slice(hd * C_HD, (hd + 1) * C_HD)
            mk_ref[b, :, hd, :] = mk[rows, hs]
            mv_ref[b, :, hd, :] = mv[rows, hs]
    mk16_ref[...] = mk.astype(BF16)
    mv16_ref[...] = mv.astype(BF16)


def _memkv(mem, g_mem, w_kv):
    batch = mem.shape[0]
    rows = MEM_BATCHES * N_MEM
    out5 = jax.ShapeDtypeStruct((DEPTH, batch, N_MEM, C_HEADS, C_HD), F32)
    out16 = jax.ShapeDtypeStruct((DEPTH, batch * N_MEM, D_MODEL), BF16)
    spec5 = pl.BlockSpec((None, MEM_BATCHES, N_MEM, C_HEADS, C_HD),
                         lambda b, l: (l, b, 0, 0, 0))
    spec16 = pl.BlockSpec((None, rows, D_MODEL), lambda b, l: (l, b, 0))
    w_layer = lambda b, l: jnp.where(b == 0, l, DEPTH - 1)
    return pl.pallas_call(
        _memkv_kernel,
        grid=(batch // MEM_BATCHES, DEPTH),
        in_specs=[
            pl.BlockSpec((MEM_BATCHES, N_MEM, D_MODEL), lambda b, l: (b, 0, 0)),
            pl.BlockSpec((None, 1, D_MODEL), lambda b, l: (l, 0, 0)),
            pl.BlockSpec((None, D_MODEL, 2 * D_MODEL), lambda b, l: (w_layer(b, l), 0, 0)),
        ],
        out_specs=[spec5, spec5, spec16, spec16],
        out_shape=[out5, out5, out16, out16],
        scratch_shapes=[pltpu.VMEM((DEPTH, D_MODEL, 2 * D_MODEL), BF16)],
        compiler_params=_params(("arbitrary", "arbitrary")),
        name="memkv",
    )(mem, g_mem, w_kv)


def _prompt_a_kernel(x_ref, gn_ref, wa_ref, lng_ref, lnb_ref, ws_ref, bs_ref,
                     wg_ref, bg_ref, wbr_ref, qkd_ref, sv_ref, s0_ref, *rest,
                     aliased, cast_rows):
    rest = list(rest)
    s_scr = rest.pop()
    if cast_rows:
        wout_next, wbr_next = rest.pop(), rest.pop()
    so_ref, sn_ref, m_ref, h_ref = rest.pop(), rest.pop(), rest.pop(), rest.pop()
    if aliased:
        rest.pop()
    if cast_rows:
        wout_src, wbr_src = rest.pop(), rest.pop()

        @pl.when(pl.program_id(0) < cast_rows)
        def _():
            wbr_next[...] = wbr_src[...].astype(BF16)

        @pl.when(pl.program_id(0) >= cast_rows)
        def _():
            wout_next[...] = wout_src[...].astype(BF16)

    _sample_gla_tile(qkd_ref, sv_ref, s0_ref, sn_ref, so_ref)
    n_chunks = TM // CHUNK_A
    h = _rms(x_ref[...], gn_ref[...]).astype(BF16)
    h_ref[...] = h
    v = _gelu(_dot(h, wa_ref[:, D_MODEL:2 * D_MODEL]))
    vn = _layernorm(v, lng_ref[...], lnb_ref[...]).astype(BF16)
    row = lax.broadcasted_iota(jnp.int32, (CHUNK_A, CHUNK_A), 0)
    col = lax.broadcasted_iota(jnp.int32, (CHUNK_A, CHUNK_A), 1)
    causal = row >= col
    for hd in range(A_HEADS):
        cs = slice(hd * A_HEAD_DIM, (hd + 1) * A_HEAD_DIM)
        w = jnp.where(causal, ws_ref[hd], 0.0).astype(BF16)
        vh = jnp.concatenate(
            [vn[c * CHUNK_A:(c + 1) * CHUNK_A, cs] for c in range(n_chunks)], axis=1)
        mixed = _dot(w, vh)
        for c in range(n_chunks):
            s_scr[c * CHUNK_A:(c + 1) * CHUNK_A, cs] = mixed[:, c * CHUNK_A:(c + 1) * CHUNK_A]
    u = _gelu(_dot(h, wa_ref[:, :D_MODEL]))
    z = _dot(h, wa_ref[:, 2 * D_MODEL:])
    bias = jnp.concatenate([bs_ref[...]] * n_chunks, axis=0)
    out_a = (u * (s_scr[...] + bias) * _silu(z)).astype(BF16)
    gate = _sigmoid(_dot(h, wg_ref[...]) + bg_ref[...])
    m_ref[...] = gate * _dot(out_a, wbr_ref[...])


def _prompt_a(x, qkd, sv, s0_all, s_all, p, l, w_branch=None, w_out=None):
    n = x.shape[0]
    steps = n // TM
    assert qkd.shape[0] == steps, "one sample batch tile per prompt grid step"
    tile = lambda i: (i, 0)
    first = s_all is None
    n_slabs = s0_all.shape[0] if first else 1
    assert not first or l == 0
    state_blk = (BT_GLA, B_HEADS, B_HK, B_HV)
    args = [x, p["gn"], p["wa"], p["lng"], p["lnb"], p["ws"], p["bs_full"],
            p["wg"], p["bg"], p["wbr"], qkd, sv, s0_all]
    in_specs = [
        pl.BlockSpec((TM, D_MODEL), tile),
        _layer(p["gn"], l), _layer(p["wa"], l), _layer(p["lng"], l),
        _layer(p["lnb"], l), _layer(p["ws"], l), _layer(p["bs_full"], l),
        _layer(p["wg"], l, D_MODEL, 0), _layer(p["bg"], l, D_MODEL, 0),
        _layer(p["wbr"], 0),
        pl.BlockSpec((None, GLA_ROWS, B_HK), lambda i: (i, 0, 0)),
        pl.BlockSpec((None, BT_GLA, D_MODEL), lambda i: (i, 0, 0)),
        pl.BlockSpec((None,) + state_blk, lambda i: (l, i, 0, 0, 0)),
    ]
    out_specs = [pl.BlockSpec((TM, D_MODEL), tile), pl.BlockSpec((TM, D_MODEL), tile),
                 pl.BlockSpec((n_slabs,) + state_blk, lambda i: (l, i, 0, 0, 0)),
                 pl.BlockSpec((None, BT_GLA, D_MODEL), lambda i: (i, 0, 0))]
    out_shape = [jax.ShapeDtypeStruct((n, D_MODEL), BF16),
                 jax.ShapeDtypeStruct((n, D_MODEL), F32),
                 jax.ShapeDtypeStruct(s0_all.shape, F32),
                 jax.ShapeDtypeStruct(sv.shape, F32)]
    cast_rows = 0
    if w_branch is not None:
        blk = (w_branch.shape[1] + w_out.shape[1]) // steps
        cast_rows = w_branch.shape[1] // blk
        assert blk * steps == w_branch.shape[1] + w_out.shape[1] and blk % 16 == 0
        br_blk = lambda i: jnp.minimum(i, cast_rows - 1)
        out_blk = lambda i: jnp.maximum(i - cast_rows, 0)
        args += [w_branch, w_out]
        in_specs += [pl.BlockSpec((None, blk, D_MODEL), lambda i: (l + 1, br_blk(i), 0)),
                     pl.BlockSpec((None, blk, D_MODEL), lambda i: (l + 1, out_blk(i), 0))]
        out_specs += [pl.BlockSpec((blk, D_MODEL), lambda i: (br_blk(i), 0)),
                      pl.BlockSpec((None, blk, D_MODEL), lambda i: (0, out_blk(i), 0))]
        out_shape += [jax.ShapeDtypeStruct(w_branch.shape[1:], BF16),
                      jax.ShapeDtypeStruct((1,) + w_out.shape[1:], BF16)]
    aliases = {}
    if not first:
        in_specs.append(pl.BlockSpec(memory_space=pl.ANY))
        args.append(s_all)
        aliases = {len(args) - 1: 2}
    return pl.pallas_call(
        functools.partial(_prompt_a_kernel, aliased=not first, cast_rows=cast_rows),
        grid=(steps,),
        in_specs=in_specs,
        out_specs=out_specs,
        out_shape=out_shape,
        scratch_shapes=[pltpu.VMEM((TM, D_MODEL), F32)],
        input_output_aliases=aliases,
        compiler_params=_params(("arbitrary",)),
        name="prompt_a",
    )(*args)


def _prompt_b_kernel(h_ref, ma_ref, wb_ref, wgup_ref, bgu_ref, gnf_ref,
                     wg_ref, bg_ref, wbr_ref, *rest, split_every):
    if split_every:
        wt_ref, m_ref, st_ref, *split_out, o_scr = rest
        step = pl.program_id(0) * pl.num_programs(1) + pl.program_id(1)

        @pl.when(step % split_every == 0)
        def _():
            _split_w_in_kernel(wt_ref, *split_out)
    else:
        m_ref, st_ref, o_scr = rest

    @pl.when(pl.program_id(1) == 0)
    def _():
        st_ref[...] = jnp.zeros_like(st_ref)

    h = h_ref[...]
    q = _dot(h, wb_ref[:, 0:B_KEY_DIM]) * (B_HK ** -0.5)
    k = _dot(h, wb_ref[:, B_KEY_DIM:2 * B_KEY_DIM])
    v = _dot(h, wb_ref[:, 2 * B_KEY_DIM:2 * B_KEY_DIM + D_MODEL])
    gd = _dot(h, wb_ref[:, 2 * B_KEY_DIM + 2 * D_MODEL:]).astype(BF16)
    logit = _dot(gd, wgup_ref[...]) + bgu_ref[...]
    z_off = 2 * B_KEY_DIM + D_MODEL
    z = _dot(h, wb_ref[:, z_off:z_off + D_MODEL])
    gate_logit = _dot(h, wg_ref[...]) + bg_ref[...]

    n_chunks = TM // CHUNK_B
    chunk_rows = [slice(c * CHUNK_B, (c + 1) * CHUNK_B) for c in range(n_chunks)]
    row = lax.broadcasted_iota(jnp.int32, (CHUNK_B, CHUNK_B), 0)
    col = lax.broadcasted_iota(jnp.int32, (CHUNK_B, CHUNK_B), 1)
    causal = row >= col
    tri = jnp.where(causal, 1.0, 0.0).astype(BF16)

    g = _log_sigmoid(logit) * INV_GATE_NORMALIZER
    g_hi = g.astype(BF16)
    g_lo = (g - g_hi.astype(F32)).astype(BF16)
    bcum = jnp.concatenate(
        [_dot(tri, g_hi[rs]) + _dot(tri, g_lo[rs]) for rs in chunk_rows], axis=0)
    bcum3 = bcum.reshape(n_chunks, CHUNK_B, B_KEY_DIM)
    b_last3 = bcum3[:, CHUNK_B - 1:CHUNK_B, :]
    q_dec = (q * jnp.exp(bcum)).astype(BF16)
    k_dec = (k * jnp.exp(-bcum)).astype(BF16)
    k_tail = (k.reshape(bcum3.shape) * jnp.exp(b_last3 - bcum3)).astype(BF16).reshape(k.shape)
    decay3 = jnp.exp(b_last3)
    vb = v.astype(BF16)

    scores, kv_t = {}, {}
    for c, rs in enumerate(chunk_rows):
        for hd in range(B_HEADS):
            ks = slice(hd * B_HK, (hd + 1) * B_HK)
            vs = slice(hd * B_HV, (hd + 1) * B_HV)
            scores[c, hd] = jnp.where(
                causal, _dot_nt(q_dec[rs, ks], k_dec[rs, ks]), 0.0).astype(BF16)
            kv_t[c, hd] = _dot_tn(vb[rs, vs], k_tail[rs, ks])
    silu_z = _silu(z)
    gate = _sigmoid(gate_logit)

    for hd in range(B_HEADS):
        ks = slice(hd * B_HK, (hd + 1) * B_HK)
        vs = slice(hd * B_HV, (hd + 1) * B_HV)
        state_t = st_ref[hd]
        for c, rs in enumerate(chunk_rows):
            o_scr[rs, vs] = (_dot(scores[c, hd], vb[rs, vs])
                             + _dot_nt(q_dec[rs, ks], state_t.astype(BF16)))
            state_t = state_t * decay3[c][:, ks] + kv_t[c, hd]
        st_ref[hd] = state_t

    out_b = (_head_rms(o_scr[...], gnf_ref[...]) * silu_z).astype(BF16)
    m_ref[...] = ma_ref[...] + gate * _dot(out_b, wbr_ref[...])


def _prompt_b(h, ma, p, l, batch, w_in_t=None):
    n = h.shape[0]
    steps = n // batch // TM
    tile = lambda b, t: (b * steps + t, 0)
    args = [h, ma, p["wb"], p["wgup"], p["bgu"], p["gnf"], p["wg"], p["bg"], p["wbr"]]
    in_specs = [
        pl.BlockSpec((TM, D_MODEL), tile), pl.BlockSpec((TM, D_MODEL), tile),
        _layer(p["wb"], l), _layer(p["wgup"], l), _layer(p["bgu"], l),
        _layer(p["gnf"], l),
        _layer(p["wg"], l, D_MODEL, 1), _layer(p["bg"], l, D_MODEL, 1),
        _layer(p["wbr"], 1),
    ]
    out_specs = [
        pl.BlockSpec((TM, D_MODEL), tile),
        pl.BlockSpec((None, B_HEADS, B_HV, B_HK), lambda b, t: (b, 0, 0, 0)),
    ]
    out_shape = [jax.ShapeDtypeStruct((n, D_MODEL), F32),
                 jax.ShapeDtypeStruct((batch, B_HEADS, B_HV, B_HK), F32)]
    split_every = 0
    if w_in_t is not None:
        slabs = w_in_t.shape[2] // TR_PREP
        split_every = batch * steps // slabs
        assert split_every * slabs == batch * steps
        slab = lambda b, t: (b * steps + t) // split_every
        args.append(w_in_t)
        in_specs.append(pl.BlockSpec((None, IN_COLS, TR_PREP), lambda b, t: (l + 1, 0, slab(b, t))))
        out_specs += [pl.BlockSpec((None, TR_PREP, c), lambda b, t: (0, slab(b, t), 0))
                      for c in SPLIT_WIDTHS]
        out_shape += [jax.ShapeDtypeStruct((1, w_in_t.shape[2], c), BF16) for c in SPLIT_WIDTHS]
    res = pl.pallas_call(
        functools.partial(_prompt_b_kernel, split_every=split_every),
        grid=(batch, steps),
        in_specs=in_specs,
        out_specs=out_specs,
        out_shape=out_shape,
        scratch_shapes=[pltpu.VMEM((TM, D_MODEL), F32)],
        compiler_params=_params(("arbitrary", "arbitrary")),
        name="prompt_b",
    )(*args)
    return res[0], res[1], tuple(res[2:])


def _prompt_c_kernel(x_ref, h_ref, mab_ref, mk_ref, mv_ref, wc_ref,
                     wg_ref, bg_ref, wbr_ref, wout_ref, gf_ref, cq_ref, ck_ref, cv_ref,
                     xo_ref, oc_ref, *, final):
    h = h_ref[...]
    q = _dot(h, wc_ref[:, :D_MODEL]).astype(BF16)
    z = _dot(h, wc_ref[:, D_MODEL:])
    gate_logit = _dot(h, wg_ref[...]) + bg_ref[...]
    mk = mk_ref[...]
    mv = mv_ref[...]
    heads = [slice(hd * C_HD, (hd + 1) * C_HD) for hd in range(C_HEADS)]
    scores = [_dot_nt(q[:, hs], mk[:, hs]) * (C_HD ** -0.5) for hs in heads]
    probs = []
    for s in scores:
        e = jnp.exp(s - jnp.max(s, axis=-1, keepdims=True))
        probs.append((e / jnp.sum(e, axis=-1, keepdims=True)).astype(BF16))
    parts = [_dot(p, mv[:, hs]) for p, hs in zip(probs, heads)]
    out_c = (jnp.concatenate(parts, axis=1) * _silu(z)).astype(BF16)
    merged = mab_ref[...] + _sigmoid(gate_logit) * _dot(out_c, wbr_ref[...])
    x_new = x_ref[...] + _dot(merged.astype(BF16), wout_ref[...])
    xo_ref[...] = _rms(x_new, gf_ref[...]) if final else x_new
    _sample_att_tile(cq_ref, ck_ref, cv_ref, oc_ref)


def _prompt_c(x, h, mab, mk_all, mv_all, cq_packed, cache_k, cache_v, p, l, batch):
    n = x.shape[0]
    steps = n // batch // TM
    dec_batch = cq_packed.shape[0]
    assert dec_batch == batch * steps * BT_ATT, "one sample batch tile per prompt grid step"
    tile = lambda b, t: (b * steps + t, 0)
    mem = lambda b, t: (l, b, 0)
    att_row = lambda b, t: (b * steps + t, 0, 0)
    att_cache = lambda b, t: (l, b * steps + t, 0, 0, 0)
    args = (x, h, mab, mk_all, mv_all, p["wc"], p["wg"], p["bg"], p["wbr"], p["wout"], p["gf"],
            cq_packed, cache_k, cache_v)
    return pl.pallas_call(
        functools.partial(_prompt_c_kernel, final=l == DEPTH - 1),
        grid=(batch, steps),
        in_specs=[
            pl.BlockSpec((TM, D_MODEL), tile), pl.BlockSpec((TM, D_MODEL), tile),
            pl.BlockSpec((TM, D_MODEL), tile),
            pl.BlockSpec((None, N_MEM, D_MODEL), mem), pl.BlockSpec((None, N_MEM, D_MODEL), mem),
            _layer(p["wc"], l),
            _layer(p["wg"], l, D_MODEL, 2), _layer(p["bg"], l, D_MODEL, 2),
            _layer(p["wbr"], 2), _layer(p["wout"], l),
            _resident(p["gf"].shape),
            pl.BlockSpec((BT_ATT, PACK_ROWS, LANES), att_row),
            pl.BlockSpec((None, BT_ATT, N_MEM, PACK_ROWS, LANES), att_cache),
            pl.BlockSpec((None, BT_ATT, N_MEM, PACK_ROWS, LANES), att_cache),
        ],
        out_specs=[pl.BlockSpec((TM, D_MODEL), tile),
                   pl.BlockSpec((BT_ATT, PACK_ROWS, LANES), att_row)],
        out_shape=[jax.ShapeDtypeStruct((n, D_MODEL), F32),
                   jax.ShapeDtypeStruct((dec_batch, PACK_ROWS, LANES), F32)],
        compiler_params=_params(("parallel", "arbitrary")),
        name="prompt_c",
    )(*args)


def _sample_front_kernel(x_ref, gn_ref, wa_ref, wb_ref, wc_ref, lng_ref, lnb_ref,
                         ws0_ref, bs0_ref, wgup_ref, bgu_ref,
                         h_ref, vn_ref, oa_ref, qkd_ref, v_ref, szb_ref, cqp_ref, szc_ref):
    h = _rms(x_ref[...], gn_ref[...]).astype(BF16)
    h_ref[...] = h
    u = _gelu(_dot(h, wa_ref[:, :D_MODEL]))
    vn = _layernorm(_gelu(_dot(h, wa_ref[:, D_MODEL:2 * D_MODEL])), lng_ref[...], lnb_ref[...])
    vn_ref[...] = vn
    z = _dot(h, wa_ref[:, 2 * D_MODEL:])
    oa_ref[...] = (u * (vn * ws0_ref[...] + bs0_ref[...]) * _silu(z)).astype(BF16)
    q = _dot(h, wb_ref[:, 0:B_KEY_DIM]) * (B_HK ** -0.5)
    k = _dot(h, wb_ref[:, B_KEY_DIM:2 * B_KEY_DIM])
    v_ref[...] = _dot(h, wb_ref[:, 2 * B_KEY_DIM:2 * B_KEY_DIM + D_MODEL])
    szb_ref[...] = _silu(_dot(h, wb_ref[:, 2 * B_KEY_DIM + D_MODEL:2 * B_KEY_DIM + 2 * D_MODEL]))
    gd = _dot(h, wb_ref[:, 2 * B_KEY_DIM + 2 * D_MODEL:]).astype(BF16)
    g = _log_sigmoid(_dot(gd, wgup_ref[...]) + bgu_ref[...]) * INV_GATE_NORMALIZER
    for a, arr in enumerate((q, k, jnp.exp(g))):
        for hd in range(B_HEADS):
            qkd_ref[:, a * B_HEADS + hd, :] = arr[:, hd * B_HK:(hd + 1) * B_HK]
    cq = _dot(h, wc_ref[:, :D_MODEL]) * (LOG2_E * C_HD ** -0.5)
    for r in range(PACK_ROWS):
        lo = (r % C_HEADS) * C_HD + (r // C_HEADS) * LANES
        cqp_ref[:, r, :] = cq[:, lo:lo + LANES]
    szc_ref[...] = _silu(_dot(h, wc_ref[:, D_MODEL:]))


def _sample_front(x, p, l):
    n = x.shape[0]
    wide = jax.ShapeDtypeStruct((n, D_MODEL), F32)
    wide16 = jax.ShapeDtypeStruct((n, D_MODEL), BF16)
    qkd = jax.ShapeDtypeStruct((n, 3 * B_HEADS, B_HK), F32)
    cqp = jax.ShapeDtypeStruct((n, PACK_ROWS, LANES), F32)
    out_shape = [wide16, wide, wide16, qkd, wide, wide, cqp, wide]
    names = ("gn", "wa", "wb", "wc", "lng", "lnb", "ws0", "bs0", "wgup", "bgu")
    return pl.pallas_call(
        _sample_front_kernel,
        grid=(1,),
        in_specs=[_resident(x.shape)] + [_layer(p[k], l) for k in names],
        out_specs=[pl.BlockSpec(s.shape, lambda i, nd=len(s.shape): (0,) * nd)
                   for s in out_shape],
        out_shape=out_shape,
        compiler_params=_params(("arbitrary",)),
        name="sample_front",
    )(x, *(p[k] for k in names))


GLA_ROWS = 3 * B_HEADS * BT_GLA


def _sample_gla_tile(qkd_ref, v_ref, s0_ref, s_ref, o_ref):
    pad = jnp.zeros((B_HK - GLA_ROWS, B_HK), F32)
    cols = jnp.concatenate([qkd_ref[...], pad], axis=0).T
    for j in range(BT_GLA):
        for hd in range(B_HEADS):
            vs = slice(hd * B_HV, (hd + 1) * B_HV)
            iq = hd * BT_GLA + j
            ik = (B_HEADS + hd) * BT_GLA + j
            ie = (2 * B_HEADS + hd) * BT_GLA + j
            state = (cols[:, ie:ie + 1] * s0_ref[j, hd]
                     + cols[:, ik:ik + 1] * v_ref[j:j + 1, vs])
            s_ref[0, j, hd] = state
            o_ref[j:j + 1, vs] = jnp.sum(cols[:, iq:iq + 1] * state, axis=0, keepdims=True)
    if s_ref.shape[0] > 1:
        s_ref[1:] = jnp.zeros((s_ref.shape[0] - 1,) + s_ref.shape[1:], F32)


def _stack_gla_rows(qkd):
    n, groups, _ = qkd.shape
    a = qkd.reshape(n // BT_GLA, BT_GLA, groups, B_HK)
    return jnp.swapaxes(a, 1, 2).reshape(n // BT_GLA, GLA_ROWS, B_HK)


PACK_ROWS = C_HEADS * C_HD // LANES


def _pack_heads(a):
    lead = a.shape[:-2]
    a = a.reshape(*lead, C_HEADS, C_HD // LANES, LANES)
    return jnp.swapaxes(a, -2, -3).reshape(*lead, PACK_ROWS, LANES)


def _sample_att_tile(cq_ref, mk_ref, mv_ref, o_ref):
    for j in range(BT_ATT):
        q = cq_ref[j]
        r = jnp.sum(mk_ref[j] * q[None], axis=-1, keepdims=True)
        r = jnp.broadcast_to(r, (N_MEM, PACK_ROWS, LANES))
        s = r + pltpu.roll(r, C_HEADS, 1)
        e = jnp.exp2(s - jnp.max(s, axis=0, keepdims=True))
        o_ref[j] = jnp.sum(e * mv_ref[j], axis=0) / jnp.sum(e, axis=0)


def _sample_back_kernel(x_ref, h_ref, oa_ref, og_ref, szb_ref, oc_ref, szc_ref, gnf_ref,
                        wg_ref, bg_ref, wbr_ref, wout_ref, gf_ref, xo_ref, *, final):
    h = h_ref[...]
    out_b = (_head_rms(og_ref[...], gnf_ref[...]) * szb_ref[...]).astype(BF16)
    oc = jnp.concatenate(
        [oc_ref[:, (c % 2) * C_HEADS + c // 2, :] for c in range(PACK_ROWS)], axis=1)
    out_c = (oc * szc_ref[...]).astype(BF16)
    merged = None
    for i, br in enumerate((oa_ref[...], out_b, out_c)):
        cs = slice(i * D_MODEL, (i + 1) * D_MODEL)
        gate = _sigmoid(_dot(h, wg_ref[:, cs]) + bg_ref[:, cs])
        term = gate * _dot(br, wbr_ref[i])
        merged = term if merged is None else merged + term
    x_new = x_ref[...] + _dot(merged.astype(BF16), wout_ref[...])
    xo_ref[...] = _rms(x_new, gf_ref[...]) if final else x_new


def _sample_back(x, h, oa, og, szb, oc, szc, p, l):
    acts = (x, h, oa, og, szb, oc, szc)
    branch_w = _resident(p["wbr"].shape)
    return pl.pallas_call(
        functools.partial(_sample_back_kernel, final=l == DEPTH - 1),
        grid=(1,),
        in_specs=[_resident(a.shape) for a in acts] + [
            _layer(p["gnf"], l), _layer(p["wg"], l), _layer(p["bg"], l), branch_w,
            _layer(p["wout"], l), _resident(p["gf"].shape)],
        out_specs=pl.BlockSpec(x.shape, lambda i: (0, 0)),
        out_shape=jax.ShapeDtypeStruct(x.shape, F32),
        compiler_params=_params(("arbitrary",)),
        name="sample_back",
    )(*acts, p["gnf"], p["wg"], p["bg"], p["wbr"], p["wout"], p["gf"])


def kernel(x_prompt, x_sample, state_gla, cache_mem_k, cache_mem_v, mem_prompt, g_norm, w_in, b_gate, w_s, b_s, ln_v_g, ln_v_b, w_g_up, b_g, gn_g, g_mem, w_mem_kv, w_branch, w_out, g_final):
    batch, seq, _ = x_prompt.shape
    dec_batch = x_sample.shape[0]
    rows = lambda a: a.reshape(DEPTH, 1, -1)

    w_in_t = jnp.swapaxes(w_in, 1, 2)
    w_split = _split_w_in(w_in_t, 0)
    p = {
        "wgup": jnp.pad(w_g_up, ((0, 0), (0, GATE_RANK_PAD - GATE_RANK), (0, 0))).astype(BF16),
        "gn": rows(g_norm), "lng": rows(ln_v_g), "lnb": rows(ln_v_b), "bgu": rows(b_g),
        "bg": rows(b_gate), "gnf": rows(jnp.tile(gn_g, (1, B_HEADS))),
        "ws": w_s,
        "bs_full": jnp.repeat(jnp.swapaxes(b_s, 1, 2), A_HEAD_DIM, axis=2),
        "ws0": rows(jnp.repeat(w_s[:, :, 0, 0], A_HEAD_DIM, axis=1)),
        "bs0": rows(jnp.repeat(b_s[:, :, 0], A_HEAD_DIM, axis=1)),
        "gf": g_final.reshape(1, D_MODEL),
    }

    mk_out, mv_out, mk_all, mv_all = _memkv(
        mem_prompt, g_mem.reshape(DEPTH, 1, D_MODEL), w_mem_kv)

    xp = x_prompt.reshape(batch * seq, D_MODEL)
    xs = x_sample.reshape(dec_batch, D_MODEL)
    cache_k = _pack_heads(cache_mem_k)
    cache_v = _pack_heads(cache_mem_v)
    gla_p, v_s = [], []
    gla_s = None
    w_branch_rows = w_branch.reshape(DEPTH, N_BRANCH * D_MODEL, D_MODEL)
    w_proj = (_cast_bf16(w_branch.reshape(DEPTH * N_BRANCH, D_MODEL, D_MODEL), N_BRANCH),
              _cast_bf16(w_out, 1))
    for l in range(DEPTH):
        p.update(zip(("wa", "wb", "wc", "wg"), w_split))
        p.update(zip(("wbr", "wout"), w_proj))
        (hs_, vn, oa, qkd, v, szb, cq_packed, szc) = _sample_front(xs, p, l)
        qkd = _stack_gla_rows(qkd)
        sv = v.reshape(dec_batch // BT_GLA, BT_GLA, D_MODEL)

        more = l + 1 < DEPTH
        h, m, gla_s, og, *proj_next = _prompt_a(
            xp, qkd, sv, state_gla, gla_s, p, l,
            w_branch_rows if more else None, w_out if more else None)
        m, st, w_next = _prompt_b(h, m, p, l, batch, w_in_t if more else None)
        xp, oc = _prompt_c(xp, h, m, mk_all, mv_all, cq_packed, cache_k, cache_v, p, l, batch)
        gla_p.append(jnp.swapaxes(st, -1, -2))

        og = og.reshape(dec_batch, D_MODEL)
        xs = _sample_back(xs, hs_, oa, og, szb, oc, szc, p, l)
        v_s.append(vn)
        w_split = w_next
        if more:
            w_proj = (proj_next[0].reshape(N_BRANCH, D_MODEL, D_MODEL), proj_next[1])

    y_prompt = xp.reshape(batch, seq, D_MODEL)
    y_sample = xs.reshape(dec_batch, 1, D_MODEL)
    return (y_prompt, y_sample, jnp.stack(gla_p), mk_out, mv_out, gla_s,
            jnp.stack(v_s).reshape(DEPTH, dec_batch, 1, A_HEADS, A_HEAD_DIM))
```
